```python
import math
import jax, jax.numpy as jnp
from jax import lax
import numpy as np

D_MODEL = 1024
BATCH = 16
SEQ = 4096
DEPTH = 1

DA_HEADS = 4
DA_HEAD_DIM = 64
DA_V_DIM = 2 * DA_HEAD_DIM
DA_QK_COLS = DA_HEADS * 2 * DA_HEAD_DIM
DA_V_COLS = DA_HEADS * DA_V_DIM
GDN_HEADS = 4
GDN_K_DIM = 128
GDN_V_DIM = 128
GDN_QK_COLS = GDN_HEADS * GDN_K_DIM
GDN_V_COLS = GDN_HEADS * GDN_V_DIM
GDN_CONV_CH = 2 * GDN_QK_COLS + GDN_V_COLS
SHORT_CONV = 4
CHUNK = 64
IN_SIZES = (DA_QK_COLS, DA_QK_COLS, DA_V_COLS, GDN_QK_COLS, GDN_QK_COLS, GDN_V_COLS, GDN_V_COLS, GDN_HEADS, GDN_HEADS)
IN_COLS = DA_QK_COLS * 2 + DA_V_COLS + GDN_QK_COLS * 2 + GDN_V_COLS * 2 + GDN_HEADS * 2
MIX_WIDTH = DA_V_COLS + GDN_V_COLS
D_FF = 2816
FFN_CONV = 3
Q_BLOCK = 128
ROPE_THETA = 10000.0
EPS = 1e-6

kernel_name = "hymba_diffattn_gdn_convffn"


def rms_norm(x, w):
    xf = x.astype(jnp.float32)
    y = xf * lax.rsqrt(jnp.mean(xf * xf, axis=-1, keepdims=True) + EPS)
    return (y * w.astype(jnp.float32)).astype(x.dtype)


def causal_depthwise_conv(x, w):
    k, c = w.shape
    return lax.conv_general_dilated(
        x, w[:, None, :].astype(x.dtype), window_strides=(1,), padding=[(k - 1, 0)],
        dimension_numbers=("NWC", "WIO", "NWC"), feature_group_count=c)


def apply_rope(x):
    t, d = x.shape[1], x.shape[-1]
    inv_freq = ROPE_THETA ** (-jnp.arange(0, d, 2, dtype=jnp.float32) / d)
    ang = jnp.arange(t, dtype=jnp.float32)[:, None] * inv_freq[None, :]
    shape = (t,) + (1,) * (x.ndim - 3) + (d // 2,)
    cos, sin = jnp.cos(ang).reshape(shape), jnp.sin(ang).reshape(shape)
    xf = x.astype(jnp.float32)
    x1, x2 = xf[..., : d // 2], xf[..., d // 2:]
    return jnp.concatenate([x1 * cos - x2 * sin, x2 * cos + x1 * sin], axis=-1).astype(x.dtype)


def l2_normalize(x):
    return x * lax.rsqrt(jnp.sum(x * x, axis=-1, keepdims=True) + EPS)


def diff_attention(q, k, v, lam_q1, lam_k1, lam_q2, lam_k2, subln_w, lambda_init):
    b, t, h = q.shape[0], q.shape[1], q.shape[2]
    q = apply_rope(q) * (DA_HEAD_DIM ** -0.5)
    k = apply_rope(k)
    lam = (jnp.exp(jnp.sum(lam_q1.astype(jnp.float32) * lam_k1.astype(jnp.float32)))
           - jnp.exp(jnp.sum(lam_q2.astype(jnp.float32) * lam_k2.astype(jnp.float32))) + lambda_init)
    nb = t // Q_BLOCK
    qb = q.reshape(b, nb, Q_BLOCK, h, 2, DA_HEAD_DIM).transpose(1, 0, 3, 4, 2, 5)
    kt = k.transpose(0, 2, 3, 1, 4)
    vt = v.transpose(0, 2, 1, 3)
    k_pos = jnp.arange(t)

    def one_block(args):
        q_blk, blk = args
        s = jnp.einsum("bhcqd,bhckd->bhcqk", q_blk, kt).astype(jnp.float32)
        q_pos = blk * Q_BLOCK + jnp.arange(Q_BLOCK)
        s = jnp.where(k_pos[None, :] <= q_pos[:, None], s, -jnp.inf)
        p = jax.nn.softmax(s, axis=-1)
        a = p[:, :, 0] - lam * p[:, :, 1]
        return jnp.einsum("bhqk,bhkd->bqhd", a.astype(vt.dtype), vt)

    o = lax.map(one_block, (qb, jnp.arange(nb)))
    o = o.transpose(1, 0, 2, 3, 4).reshape(b, t, h, DA_V_DIM)
    o = rms_norm(o, subln_w) * (1.0 - lambda_init)
    return o.reshape(b, t, h * DA_V_DIM)


def gated_deltanet(q, k, v, z, beta_logit, a, conv_w, a_log, dt_bias, norm_w):
    b, t = q.shape[0], q.shape[1]
    h, dk, dv, c = GDN_HEADS, GDN_K_DIM, GDN_V_DIM, CHUNK
    f32 = jnp.float32
    qkv = jax.nn.silu(causal_depthwise_conv(jnp.concatenate([q, k, v], axis=-1), conv_w))
    q, k, v = jnp.split(qkv, [GDN_QK_COLS, 2 * GDN_QK_COLS], axis=-1)
    q = l2_normalize(q.reshape(b, t, h, dk).astype(f32)) * (dk ** -0.5)
    k = l2_normalize(k.reshape(b, t, h, dk).astype(f32))
    v = v.reshape(b, t, h, dv).astype(f32)
    beta = jax.nn.sigmoid(beta_logit.astype(f32))
    g = -jnp.exp(a_log.astype(f32)) * jax.nn.softplus(a.astype(f32) + dt_bias.astype(f32))
    n = t // c

    def chunks(x):
        return x.reshape(b, n, c, h, -1).transpose(0, 3, 1, 2, 4)

    qc, kc, vc = chunks(q), chunks(k), chunks(v)
    bc = beta.reshape(b, n, c, h).transpose(0, 3, 1, 2)
    gcum = jnp.cumsum(g.reshape(b, n, c, h).transpose(0, 3, 1, 2), axis=-1)
    tril = jnp.tril(jnp.ones((c, c), dtype=bool))
    strict = jnp.tril(jnp.ones((c, c), dtype=bool), -1)
    decay_mat = jnp.exp(jnp.where(tril, gcum[..., :, None] - gcum[..., None, :], -jnp.inf))
    kb = kc * bc[..., None]
    kkt = jnp.einsum("bhncd,bhnsd->bhncs", kb, kc) * decay_mat
    a_mat = jnp.eye(c, dtype=f32) + jnp.where(strict, kkt, 0.0)
    rhs = jnp.concatenate([vc * bc[..., None], kb * jnp.exp(gcum)[..., None]], axis=-1)
    sol = lax.linalg.triangular_solve(a_mat, rhs, left_side=True, lower=True, unit_diagonal=True)
    u, w = sol[..., :dv], sol[..., dv:]
    qk = jnp.einsum("bhncd,bhnsd->bhncs", qc, kc) * decay_mat
    q_dec = qc * jnp.exp(gcum)[..., None]
    k_dec = kc * jnp.exp(gcum[..., -1:] - gcum)[..., None]
    g_last = jnp.exp(gcum[..., -1])

    def step(state, xs):
        u_i, w_i, qk_i, qd_i, kd_i, gl_i = xs
        v_new = u_i - jnp.einsum("bhcd,bhdv->bhcv", w_i, state)
        o_i = jnp.einsum("bhcd,bhdv->bhcv", qd_i, state) + jnp.einsum("bhcs,bhsv->bhcv", qk_i, v_new)
        state = state * gl_i[..., None, None] + jnp.einsum("bhcd,bhcv->bhdv", kd_i, v_new)
        return state, o_i

    xs = tuple(jnp.moveaxis(x_, 2, 0) for x_ in (u, w, qk, q_dec, k_dec, g_last))
    s0 = jnp.zeros((b, h, dk, dv), f32)
    _, o = lax.scan(step, s0, xs)
    o = o.transpose(1, 0, 3, 2, 4).reshape(b, t, h, dv).astype(z.dtype)
    o = rms_norm(o, norm_w) * jax.nn.silu(z.reshape(b, t, h, dv))
    return o.reshape(b, t, h * dv)


def conv_glu_ffn(x, w_up, conv_w, w_down):
    hid = jnp.einsum("btd,df->btf", x, w_up)
    hid = causal_depthwise_conv(hid, conv_w)
    gate, up = jnp.split(hid, 2, axis=-1)
    return jnp.einsum("btf,fd->btd", jax.nn.silu(gate) * up, w_down)


def setup_inputs(seed: int = 0) -> dict:
    key = jax.random.key(seed)
    ks = jax.random.split(key, 18)
    L = DEPTH
    f32 = jnp.float32

    def nrm(k, shape, scale):
        return jax.random.normal(k, shape, f32) * scale

    dt = jnp.exp(jax.random.uniform(ks[10], (L, GDN_HEADS), f32, math.log(1e-3), math.log(1e-1)))
    return {
        "x": nrm(ks[0], (BATCH, SEQ, D_MODEL), 1.0),
        "attn_norm_w": 1.0 + nrm(ks[1], (L, D_MODEL), 0.02),
        "w_in": nrm(ks[2], (L, D_MODEL, IN_COLS), D_MODEL ** -0.5),
        "da_lambda_q1": nrm(ks[3], (L, DA_HEAD_DIM), 0.1),
        "da_lambda_k1": nrm(ks[4], (L, DA_HEAD_DIM), 0.1),
        "da_lambda_q2": nrm(ks[5], (L, DA_HEAD_DIM), 0.1),
        "da_lambda_k2": nrm(ks[6], (L, DA_HEAD_DIM), 0.1),
        "da_subln_w": 1.0 + nrm(ks[7], (L, DA_V_DIM), 0.02),
        "gdn_conv_w": nrm(ks[8], (L, SHORT_CONV, GDN_CONV_CH), SHORT_CONV ** -0.5),
        "gdn_a_log": jnp.log(jax.random.uniform(ks[9], (L, GDN_HEADS), f32, 1.0, 16.0)),
        "gdn_dt_bias": dt + jnp.log(-jnp.expm1(-dt)),
        "gdn_norm_w": 1.0 + nrm(ks[11], (L, GDN_V_DIM), 0.02),
        "w_out": nrm(ks[12], (L, MIX_WIDTH, D_MODEL), MIX_WIDTH ** -0.5),
        "ffn_norm_w": 1.0 + nrm(ks[13], (L, D_MODEL), 0.02),
        "ffn_w_up": nrm(ks[14], (L, D_MODEL, 2 * D_FF), D_MODEL ** -0.5),
        "ffn_conv_w": nrm(ks[15], (L, FFN_CONV, 2 * D_FF), FFN_CONV ** -0.5),
        "ffn_w_down": nrm(ks[16], (L, D_FF, D_MODEL), D_FF ** -0.5),
        "final_norm_w": 1.0 + nrm(ks[17], (D_MODEL,), 0.02),
    }


def reference(x, attn_norm_w, w_in, da_lambda_q1, da_lambda_k1, da_lambda_q2, da_lambda_k2,
              da_subln_w, gdn_conv_w, gdn_a_log, gdn_dt_bias, gdn_norm_w, w_out,
              ffn_norm_w, ffn_w_up, ffn_conv_w, ffn_w_down, final_norm_w):
    b, t = x.shape[0], x.shape[1]
    split_idx = [int(i) for i in np.cumsum(IN_SIZES)[:-1]]
    h = x
    for l in range(DEPTH):
        lambda_init = 0.8 - 0.6 * math.exp(-0.3 * l)
        xn = rms_norm(h, attn_norm_w[l])
        proj = jnp.einsum("btd,dc->btc", xn, w_in[l])
        da_q, da_k, da_v, g_q, g_k, g_v, g_z, g_b, g_a = jnp.split(proj, split_idx, axis=-1)
        y_da = diff_attention(
            da_q.reshape(b, t, DA_HEADS, 2, DA_HEAD_DIM), da_k.reshape(b, t, DA_HEADS, 2, DA_HEAD_DIM),
            da_v.reshape(b, t, DA_HEADS, DA_V_DIM), da_lambda_q1[l], da_lambda_k1[l],
            da_lambda_q2[l], da_lambda_k2[l], da_subln_w[l], lambda_init)
        y_gdn = gated_deltanet(g_q, g_k, g_v, g_z, g_b, g_a, gdn_conv_w[l], gdn_a_log[l],
                               gdn_dt_bias[l], gdn_norm_w[l])
        mix = jnp.concatenate([y_da, y_gdn.astype(y_da.dtype)], axis=-1)
        h = h + jnp.einsum("btc,cd->btd", mix, w_out[l])
        h = h + conv_glu_ffn(rms_norm(h, ffn_norm_w[l]), ffn_w_up[l], ffn_conv_w[l], ffn_w_down[l])
    return rms_norm(h, final_norm_w)
```

```python
import functools
import math

import jax
import jax.numpy as jnp
import numpy as np
from jax import lax
from jax.experimental import pallas as pl
from jax.experimental.pallas import tpu as pltpu

F32 = jnp.float32
BF16 = jnp.bfloat16

D_MODEL = 1024
DA_HEADS = 4
DA_HEAD_DIM = 64
HEAD_W = 128
GROUP_W = 512
GDN_HEADS = 4
SHORT_CONV = 4
D_FF = 2816
FFN_CONV = 3
FF_CHUNK = 256
ROPE_THETA = 10000.0
EPS = 1e-6
LAMBDA_INIT = 0.8 - 0.6 * math.exp(-0.3 * 0)
GDN_CHUNK = 128
INV_BLOCK = 16
NEG_BIG = -1e30
VMEM_LIMIT = 56 * 1024 * 1024


def _rms(x, w):
    return x * lax.rsqrt(jnp.mean(x * x, axis=-1, keepdims=True) + EPS) * w


def _dot(a, b):
    return jnp.dot(a, b, preferred_element_type=F32)


def _dot_nt(a, b):
    return lax.dot_general(a, b, (((1,), (1,)), ((), ())), preferred_element_type=F32)


def _sigmoid(x):
    return 1.0 / (1.0 + jnp.exp(-x))


def _inproj_kernel(x_ref, nw_ref, w_ref, cos_ref, sin_ref,
                   q_ref, k_ref, v_ref, gq_ref, gk_ref, gv_ref, gz_ref, gba_ref):
    xn = _rms(x_ref[0], nw_ref[...]).astype(BF16)
    cos = cos_ref[...]
    sin = sin_ref[...]

    def proj(seg, width=GROUP_W):
        return _dot(xn, w_ref[:, seg * GROUP_W: seg * GROUP_W + width])

    def rope(p, scale):
        outs = []
        for h in range(DA_HEADS):
            ph = p[:, h * HEAD_W:(h + 1) * HEAD_W]
            outs.append((ph * cos + pltpu.roll(ph, HEAD_W // 2, 1) * sin) * scale)
        return jnp.concatenate(outs, axis=-1)

    q_ref[0] = rope(proj(0), DA_HEAD_DIM ** -0.5).astype(BF16)
    k_ref[0] = rope(proj(1), 1.0).astype(BF16)
    v_ref[0] = proj(2).astype(BF16)
    gq_ref[0] = proj(3).astype(BF16)
    gk_ref[0] = proj(4).astype(BF16)
    gv_ref[0] = proj(5).astype(BF16)
    gz_ref[0] = proj(6).astype(BF16)
    gba_ref[0] = proj(7, HEAD_W)


def _inproj(x, norm_w, w_all, cos_t, sin_t, tm):
    b, t, d = x.shape
    grid = (b, t // tm)
    tok = lambda width, dt: jax.ShapeDtypeStruct((b, t, width), dt)
    tok_spec = lambda width: pl.BlockSpec((1, tm, width), lambda i, j: (i, j, 0))
    const = lambda shape: pl.BlockSpec(shape, lambda i, j: (0,) * len(shape),
                                       pipeline_mode=pl.Buffered(1))
    return pl.pallas_call(
        _inproj_kernel,
        grid=grid,
        in_specs=[tok_spec(d), const((1, d)), const(w_all.shape),
                  pl.BlockSpec((tm, HEAD_W), lambda i, j: (j, 0)),
                  pl.BlockSpec((tm, HEAD_W), lambda i, j: (j, 0))],
        out_specs=[tok_spec(GROUP_W)] * 7 + [tok_spec(HEAD_W)],
        out_shape=[tok(GROUP_W, BF16)] * 7 + [tok(HEAD_W, F32)],
        compiler_params=pltpu.CompilerParams(
            dimension_semantics=("parallel", "parallel"), vmem_limit_bytes=VMEM_LIMIT),
        name="inproj",
    )(x, norm_w, w_all, cos_t, sin_t)


def _attn_kernel(q_ref, k_ref, v_ref, lam_ref, sw_ref, o_ref,
                 m1_ref, l1_ref, a1_ref, m2_ref, l2_ref, a2_ref, *, tq):
    i = pl.program_id(2)
    q = q_ref[0]
    lane = lax.broadcasted_iota(jnp.int32, (1, HEAD_W), 1)
    first_map = (lane % 64) < 32
    zero = jnp.zeros_like(q)
    q1 = jnp.where(first_map, q, zero)
    q2 = jnp.where(first_map, zero, q)
    stats = ((q1, m1_ref, l1_ref, a1_ref), (q2, m2_ref, l2_ref, a2_ref))

    start = pl.multiple_of(i * tq, tq)
    kd = k_ref[0, pl.ds(start, tq), :]
    vd = v_ref[0, pl.ds(start, tq), :]
    row = lax.broadcasted_iota(jnp.int32, (tq, tq), 0)
    col = lax.broadcasted_iota(jnp.int32, (tq, tq), 1)
    causal = col <= row
    for qm, m_ref, l_ref, a_ref in stats:
        s = jnp.where(causal, _dot_nt(qm, kd), NEG_BIG)
        m = jnp.max(s, axis=-1, keepdims=True)
        p = jnp.exp(s - m)
        m_ref[...] = m
        l_ref[...] = jnp.sum(p, axis=-1, keepdims=True)
        a_ref[...] = _dot(p.astype(BF16), vd)

    def body(j, carry):
        st = pl.multiple_of(j * tq, tq)
        kb = k_ref[0, pl.ds(st, tq), :]
        vb = v_ref[0, pl.ds(st, tq), :]
        for qm, m_ref, l_ref, a_ref in stats:
            s = _dot_nt(qm, kb)
            m_old = m_ref[...]
            m_new = jnp.maximum(m_old, jnp.max(s, axis=-1, keepdims=True))
            alpha = jnp.exp(m_old - m_new)
            p = jnp.exp(s - m_new)
            m_ref[...] = m_new
            l_ref[...] = alpha * l_ref[...] + jnp.sum(p, axis=-1, keepdims=True)
            a_ref[...] = alpha * a_ref[...] + _dot(p.astype(BF16), vb)
        return carry

    lax.fori_loop(0, i, body, 0)

    lp = lam_ref[...]
    lam = (jnp.exp(jnp.sum(lp[0:1] * lp[1:2], axis=-1, keepdims=True))
           - jnp.exp(jnp.sum(lp[2:3] * lp[3:4], axis=-1, keepdims=True)) + LAMBDA_INIT)
    o = a1_ref[...] / l1_ref[...] - lam * (a2_ref[...] / l2_ref[...])
    o_ref[0] = (_rms(o, sw_ref[...]) * (1.0 - LAMBDA_INIT)).astype(BF16)


def _attention(q, k, v, lam_p, subln_w, tq):
    b, t, _ = q.shape
    grid = (b, DA_HEADS, t // tq)
    return pl.pallas_call(
        functools.partial(_attn_kernel, tq=tq),
        grid=grid,
        in_specs=[pl.BlockSpec((1, tq, HEAD_W), lambda bi, h, i: (bi, i, h)),
                  pl.BlockSpec((1, t, HEAD_W), lambda bi, h, i: (bi, 0, h)),
                  pl.BlockSpec((1, t, HEAD_W), lambda bi, h, i: (bi, 0, h)),
                  pl.BlockSpec(lam_p.shape, lambda bi, h, i: (0, 0)),
                  pl.BlockSpec((1, HEAD_W), lambda bi, h, i: (0, 0))],
        out_specs=pl.BlockSpec((1, tq, HEAD_W), lambda bi, h, i: (bi, i, h)),
        out_shape=jax.ShapeDtypeStruct((b, t, GROUP_W), BF16),
        scratch_shapes=[pltpu.VMEM((tq, 1), F32), pltpu.VMEM((tq, 1), F32), pltpu.VMEM((tq, HEAD_W), F32),
                        pltpu.VMEM((tq, 1), F32), pltpu.VMEM((tq, 1), F32), pltpu.VMEM((tq, HEAD_W), F32)],
        compiler_params=pltpu.CompilerParams(
            dimension_semantics=("parallel", "parallel", "arbitrary"), vmem_limit_bytes=VMEM_LIMIT),
        name="diff_attn",
    )(q, k, v, lam_p, subln_w)


def _gdn_kernel(gq_ref, gk_ref, gv_ref, gz_ref, gba_ref, cw_ref, alog_ref, dtb_ref, nw_ref, o_ref,
                xe_ref, s_ref, u_ref, w_ref, qk_ref, qd_ref, kdt_ref, gl_ref, *, tt, c):
    nh = GDN_HEADS
    nc = tt // c
    ti = pl.program_id(1)

    @pl.when(ti == 0)
    def _():
        xe_ref[0:8, :] = jnp.zeros((8, 3 * GROUP_W), F32)
        s_ref[...] = jnp.zeros_like(s_ref)

    xe_ref[8:8 + tt, 0:GROUP_W] = gq_ref[0].astype(F32)
    xe_ref[8:8 + tt, GROUP_W:2 * GROUP_W] = gk_ref[0].astype(F32)
    xe_ref[8:8 + tt, 2 * GROUP_W:3 * GROUP_W] = gv_ref[0].astype(F32)
    cw = cw_ref[...]
    y = cw[SHORT_CONV - 1:SHORT_CONV] * xe_ref[8:8 + tt, :]
    for j in range(1, SHORT_CONV):
        y = y + cw[SHORT_CONV - 1 - j:SHORT_CONV - j] * xe_ref[8 - j:8 - j + tt, :]
    xe_ref[0:8, :] = xe_ref[tt:tt + 8, :]
    y = y * _sigmoid(y)

    gba = gba_ref[0]
    beta_all = _sigmoid(gba)
    xa = gba + dtb_ref[...]
    g_all = -jnp.exp(alog_ref[...]) * (jnp.maximum(xa, 0.0) + jnp.log1p(jnp.exp(-jnp.abs(xa))))

    r_i = lax.broadcasted_iota(jnp.int32, (c, c), 0)
    c_i = lax.broadcasted_iota(jnp.int32, (c, c), 1)
    tril = c_i <= r_i
    strict = c_i < r_i
    eye = (c_i == r_i).astype(F32)
    tril_f = tril.astype(F32)
    blk_mask = [(r_i // INV_BLOCK) == (c_i // INV_BLOCK)]
    size = INV_BLOCK
    while size < c:
        blk_mask.append(((r_i // (2 * size)) == (c_i // (2 * size))) & ((r_i // size) > (c_i // size)))
        size *= 2

    for ci in range(nc):
        rows = slice(ci * c, (ci + 1) * c)
        g_c = g_all[rows]
        gcum = jnp.dot(tril_f, g_c, preferred_element_type=F32, precision=lax.Precision.HIGHEST)
        gcum_t = gcum.T
        for h in range(nh):
            cols = slice(h * HEAD_W, (h + 1) * HEAD_W)
            q = y[rows, cols]
            k = y[rows, GROUP_W + h * HEAD_W: GROUP_W + (h + 1) * HEAD_W]
            v = y[rows, 2 * GROUP_W + h * HEAD_W: 2 * GROUP_W + (h + 1) * HEAD_W]
            q = q * lax.rsqrt(jnp.sum(q * q, axis=-1, keepdims=True) + EPS) * (HEAD_W ** -0.5)
            k = k * lax.rsqrt(jnp.sum(k * k, axis=-1, keepdims=True) + EPS)
            beta = beta_all[rows, h:h + 1]
            gc = gcum[:, nh + h:nh + h + 1]
            gr = gcum_t[nh + h:nh + h + 1, :]
            g_last = gcum[c - 1:c, nh + h:nh + h + 1]
            decay = jnp.exp(jnp.where(tril, gc - gr, -jnp.inf))
            e_gc = jnp.exp(gc)
            kb = k * beta
            k_bf = k.astype(BF16)
            kkt = _dot_nt(kb.astype(BF16), k_bf) * decay
            lmat = jnp.where(strict, kkt, 0.0)
            p = jnp.where(blk_mask[0], -lmat, 0.0)
            t_inv = eye + p
            for _ in range(int(math.log2(INV_BLOCK)) - 1):
                p_bf = p.astype(BF16)
                p = _dot(p_bf, p_bf)
                t_inv = t_inv + _dot(t_inv.astype(BF16), p.astype(BF16))
            for off_mask in blk_mask[1:]:
                t_bf = t_inv.astype(BF16)
                off = jnp.where(off_mask, lmat, 0.0).astype(BF16)
                t_inv = t_inv - _dot(_dot(t_bf, off).astype(BF16), t_bf)
            rhs = jnp.concatenate([v * beta, kb * e_gc], axis=-1).astype(BF16)
            sol = _dot(t_inv.astype(BF16), rhs)
            idx = ci * nh + h
            u_ref[idx] = sol[:, :HEAD_W]
            w_ref[idx] = sol[:, HEAD_W:].astype(BF16)
            qk_ref[idx] = (_dot_nt(q.astype(BF16), k_bf) * decay).astype(BF16)
            qd_ref[idx] = (q * e_gc).astype(BF16)
            kdt_ref[idx] = (k * jnp.exp(g_last - gc)).T.astype(BF16)
            gl_ref[idx] = jnp.broadcast_to(jnp.exp(g_last), (8, HEAD_W))

    nw = nw_ref[...]
    for ci in range(nc):
        rows = slice(ci * c, (ci + 1) * c)
        for h in range(nh):
            idx = ci * nh + h
            cols = slice(h * HEAD_W, (h + 1) * HEAD_W)
            s = s_ref[h]
            s_bf = s.astype(BF16)
            v_new = u_ref[idx] - _dot(w_ref[idx], s_bf)
            v_bf = v_new.astype(BF16)
            o = _dot(qd_ref[idx], s_bf) + _dot(qk_ref[idx], v_bf)
            s_ref[h] = s * gl_ref[idx][0:1, :] + _dot(kdt_ref[idx], v_bf)
            z = gz_ref[0, rows, cols].astype(F32)
            o_ref[0, rows, cols] = (_rms(o, nw) * (z * _sigmoid(z))).astype(BF16)


def _gdn(gq, gk, gv, gz, gba, conv_w, alog_row, dtb_row, norm_w, tt, c):
    b, t, _ = gq.shape
    nidx = (tt // c) * GDN_HEADS
    tok_spec = lambda width: pl.BlockSpec((1, tt, width), lambda i, j: (i, j, 0))
    const = lambda shape: pl.BlockSpec(shape, lambda i, j: (0,) * len(shape))
    return pl.pallas_call(
        functools.partial(_gdn_kernel, tt=tt, c=c),
        grid=(b, t // tt),
        in_specs=[tok_spec(GROUP_W)] * 4 + [tok_spec(HEAD_W), const(conv_w.shape),
                                            const((1, HEAD_W)), const((1, HEAD_W)), const((1, HEAD_W))],
        out_specs=tok_spec(GROUP_W),
        out_shape=jax.ShapeDtypeStruct((b, t, GROUP_W), BF16),
        scratch_shapes=[pltpu.VMEM((8 + tt, 3 * GROUP_W), F32),
                        pltpu.VMEM((GDN_HEADS, HEAD_W, HEAD_W), F32),
                        pltpu.VMEM((nidx, c, HEAD_W), F32),
                        pltpu.VMEM((nidx, c, HEAD_W), BF16),
                        pltpu.VMEM((nidx, c, c), BF16),
                        pltpu.VMEM((nidx, c, HEAD_W), BF16),
                        pltpu.VMEM((nidx, HEAD_W, c), BF16),
                        pltpu.VMEM((nidx, 8, HEAD_W), F32)],
        compiler_params=pltpu.CompilerParams(
            dimension_semantics=("parallel", "arbitrary"), vmem_limit_bytes=VMEM_LIMIT),
        name="gdn",
    )(gq, gk, gv, gz, gba, conv_w, alog_row, dtb_row, norm_w)


def _ffn_kernel(x_ref, yda_ref, ygdn_ref, wout_ref, fnw_ref, wup_ref, cw_ref, wdown_ref, finw_ref, o_ref,
                hx_ref, carry_ref, *, tm):
    n_chunks = D_FF // FF_CHUNK
    cw2 = 2 * FF_CHUNK

    @pl.when(pl.program_id(1) == 0)
    def _():
        carry_ref[...] = jnp.zeros_like(carry_ref)

    h = (x_ref[0] + _dot(yda_ref[0], wout_ref[0:GROUP_W, :])
         + _dot(ygdn_ref[0], wout_ref[GROUP_W:2 * GROUP_W, :]))
    hn = _rms(h, fnw_ref[...]).astype(BF16)
    acc = jnp.zeros((tm, D_MODEL), F32)
    for ci in range(n_chunks):
        cols = slice(ci * cw2, (ci + 1) * cw2)
        hx_ref[0:8, :] = carry_ref[:, cols]
        hx_ref[8:8 + tm, :] = _dot(hn, wup_ref[:, cols])
        carry_ref[:, cols] = hx_ref[tm:tm + 8, :]
        cw = cw_ref[:, cols]
        y = cw[FFN_CONV - 1:FFN_CONV] * hx_ref[8:8 + tm, :]
        for j in range(1, FFN_CONV):
            y = y + cw[FFN_CONV - 1 - j:FFN_CONV - j] * hx_ref[8 - j:8 - j + tm, :]
        gate = y[:, :FF_CHUNK]
        act = (gate * _sigmoid(gate) * y[:, FF_CHUNK:]).astype(BF16)
        acc = acc + _dot(act, wdown_ref[ci * FF_CHUNK:(ci + 1) * FF_CHUNK, :])
    o_ref[0] = _rms(h + acc, finw_ref[...])


def _ffn(x, y_da, y_gdn, w_out, ffn_norm_w, w_up, conv_w, w_down, final_norm_w, tm):
    b, t, d = x.shape
    tok_spec = lambda width: pl.BlockSpec((1, tm, width), lambda i, j: (i, j, 0))
    const = lambda shape: pl.BlockSpec(shape, lambda i, j: (0,) * len(shape),
                                       pipeline_mode=pl.Buffered(1))
    return pl.pallas_call(
        functools.partial(_ffn_kernel, tm=tm),
        grid=(b, t // tm),
        in_specs=[tok_spec(d), tok_spec(GROUP_W), tok_spec(GROUP_W), const(w_out.shape), const((1, d)),
                  const(w_up.shape), const(conv_w.shape), const(w_down.shape), const((1, d))],
        out_specs=tok_spec(d),
        out_shape=jax.ShapeDtypeStruct((b, t, d), F32),
        scratch_shapes=[pltpu.VMEM((8 + tm, 2 * FF_CHUNK), F32),
                        pltpu.VMEM((8, 2 * D_FF), F32)],
        compiler_params=pltpu.CompilerParams(
            dimension_semantics=("parallel", "arbitrary"), vmem_limit_bytes=VMEM_LIMIT),
        name="outproj_ffn",
    )(x, y_da, y_gdn, w_out, ffn_norm_w, w_up, conv_w, w_down, final_norm_w)


def _qk_col_perm():
    half = DA_HEAD_DIM // 2
    perm = []
    for h in range(DA_HEADS):
        for hi in range(2):
            for cmap in range(2):
                base = h * HEAD_W + cmap * DA_HEAD_DIM + hi * half
                perm.extend(range(base, base + half))
    return np.asarray(perm, dtype=np.int32)


def _ffn_col_perm():
    perm = []
    for ci in range(D_FF // FF_CHUNK):
        perm.extend(range(ci * FF_CHUNK, (ci + 1) * FF_CHUNK))
        perm.extend(range(D_FF + ci * FF_CHUNK, D_FF + (ci + 1) * FF_CHUNK))
    return np.asarray(perm, dtype=np.int32)


def _rope_tables(t):
    half = DA_HEAD_DIM // 2
    inv_freq = ROPE_THETA ** (-jnp.arange(0, DA_HEAD_DIM, 2, dtype=F32) / DA_HEAD_DIM)
    ang = jnp.arange(t, dtype=F32)[:, None] * inv_freq[None, :]
    cos, sin = jnp.cos(ang), jnp.sin(ang)
    assert cos.shape == (t, half)
    return jnp.tile(cos, (1, 4)), jnp.concatenate([-sin, -sin, sin, sin], axis=-1)


def _tile(t, pref):
    while t % pref:
        pref //= 2
    return pref


def kernel(x, attn_norm_w, w_in, da_lambda_q1, da_lambda_k1, da_lambda_q2, da_lambda_k2, da_subln_w,
           gdn_conv_w, gdn_a_log, gdn_dt_bias, gdn_norm_w, w_out, ffn_norm_w, ffn_w_up, ffn_conv_w,
           ffn_w_down, final_norm_w):
    b, t, d = x.shape
    assert d == D_MODEL and t % GDN_CHUNK == 0
    l = 0
    w = w_in[l]
    perm = _qk_col_perm()
    nh = GDN_HEADS
    small = jnp.pad(w[:, 7 * GROUP_W:], ((0, 0), (0, HEAD_W - 2 * nh)))
    w_all = jnp.concatenate([w[:, :GROUP_W][:, perm], w[:, GROUP_W:2 * GROUP_W][:, perm],
                             w[:, 2 * GROUP_W:7 * GROUP_W], small], axis=-1).astype(BF16)
    cos_t, sin_t = _rope_tables(t)

    q, k, v, gq, gk, gv, gz, gba = _inproj(x, attn_norm_w[l][None, :], w_all, cos_t, sin_t, _tile(t, 512))

    lam_p = jnp.stack([da_lambda_q1[l], da_lambda_k1[l], da_lambda_q2[l], da_lambda_k2[l]])
    y_da = _attention(q, k, v, lam_p, da_subln_w[l][None, :], _tile(t, 512))

    lane_pad = lambda vec: jnp.pad(vec, (nh, HEAD_W - 2 * nh))[None, :]
    y_gdn = _gdn(gq, gk, gv, gz, gba, gdn_conv_w[l], lane_pad(gdn_a_log[l]), lane_pad(gdn_dt_bias[l]),
                 gdn_norm_w[l][None, :], _tile(t, 512), GDN_CHUNK)

    fperm = _ffn_col_perm()
    return _ffn(x, y_da, y_gdn, w_out[l].astype(BF16), ffn_norm_w[l][None, :],
                ffn_w_up[l][:, fperm].astype(BF16), ffn_conv_w[l][:, fperm],
                ffn_w_down[l].astype(BF16), final_norm_w[None, :], _tile(t, 512))
```

```python
import functools
import math

import jax
import jax.numpy as jnp
import numpy as np
from jax import lax
from jax.experimental import pallas as pl
from jax.experimental.pallas import tpu as pltpu

F32 = jnp.float32
BF16 = jnp.bfloat16

D_MODEL = 1024
DA_HEADS = 4
DA_HEAD_DIM = 64
HEAD_W = 128
GROUP_W = 512
GDN_HEADS = 4
SHORT_CONV = 4
D_FF = 2816
FFN_CONV = 3
FF_CHUNK = 256
ROPE_THETA = 10000.0
EPS = 1e-6
LAMBDA_INIT = 0.8 - 0.6 * math.exp(-0.3 * 0)
GDN_CHUNK = 128
INV_BLOCK = 16
GDN_GROUP = 2
NEG_BIG = -1e30
LOG2E = math.log2(math.e)
VMEM_LIMIT = 56 * 1024 * 1024


def _rms(x, w):
    return x * lax.rsqrt(jnp.mean(x * x, axis=-1, keepdims=True) + EPS) * w


def _dot(a, b):
    return jnp.dot(a, b, preferred_element_type=F32)


def _dot_nt(a, b):
    return lax.dot_general(a, b, (((1,), (1,)), ((), ())), preferred_element_type=F32)


def _sigmoid(x):
    return 1.0 / (1.0 + jnp.exp(-x))


def _inproj_kernel(x_ref, nw_ref, w_ref, cos_ref, sin_ref,
                   q_ref, k_ref, v_ref, gq_ref, gk_ref, gv_ref, gz_ref, gba_ref):
    xn = _rms(x_ref[0], nw_ref[...]).astype(BF16)
    cos = cos_ref[...]
    sin = sin_ref[...]

    def proj(seg, width=GROUP_W):
        return _dot(xn, w_ref[:, seg * GROUP_W: seg * GROUP_W + width])

    def rope(p, scale):
        outs = []
        for h in range(DA_HEADS):
            ph = p[:, h * HEAD_W:(h + 1) * HEAD_W]
            outs.append((ph * cos + pltpu.roll(ph, HEAD_W // 2, 1) * sin) * scale)
        return jnp.concatenate(outs, axis=-1)

    q_ref[0] = rope(proj(0), DA_HEAD_DIM ** -0.5 * LOG2E).astype(BF16)
    k_ref[0] = rope(proj(1), 1.0).astype(BF16)
    v_ref[0] = proj(2).astype(BF16)
    gq_ref[0] = proj(3).astype(BF16)
    gk_ref[0] = proj(4).astype(BF16)
    gv_ref[0] = proj(5).astype(BF16)
    gz_ref[0] = proj(6).astype(BF16)
    gba_ref[0] = proj(7, HEAD_W)


def _inproj(x, norm_w, w_all, cos_t, sin_t, tm):
    b, t, d = x.shape
    grid = (b, t // tm)
    tok = lambda width, dt: jax.ShapeDtypeStruct((b, t, width), dt)
    tok_spec = lambda width: pl.BlockSpec((1, tm, width), lambda i, j: (i, j, 0))
    const = lambda shape: pl.BlockSpec(shape, lambda i, j: (0,) * len(shape),
                                       pipeline_mode=pl.Buffered(1))
    return pl.pallas_call(
        _inproj_kernel,
        grid=grid,
        in_specs=[tok_spec(d), const((1, d)), const(w_all.shape),
                  pl.BlockSpec((tm, HEAD_W), lambda i, j: (j, 0)),
                  pl.BlockSpec((tm, HEAD_W), lambda i, j: (j, 0))],
        out_specs=[tok_spec(GROUP_W)] * 7 + [tok_spec(HEAD_W)],
        out_shape=[tok(GROUP_W, BF16)] * 7 + [tok(HEAD_W, F32)],
        compiler_params=pltpu.CompilerParams(
            dimension_semantics=("parallel", "parallel"), vmem_limit_bytes=VMEM_LIMIT),
        name="inproj",
    )(x, norm_w, w_all, cos_t, sin_t)


def _attn_kernel(q_ref, k_ref, v_ref, lam_ref, sw_ref, o_ref,
                 m1_ref, l1_ref, a1_ref, m2_ref, l2_ref, a2_ref, *, tq, tk):
    i = pl.program_id(2)
    q = q_ref[0]
    lane = lax.broadcasted_iota(jnp.int32, (1, HEAD_W), 1)
    first_map = (lane % 64) < 32
    zero = jnp.zeros_like(q)
    q1 = jnp.where(first_map, q, zero)
    q2 = jnp.where(first_map, zero, q)
    stats = ((q1, m1_ref, l1_ref, a1_ref), (q2, m2_ref, l2_ref, a2_ref))
    for _, m_ref, l_ref, a_ref in stats:
        m_ref[...] = jnp.full_like(m_ref, NEG_BIG)
        l_ref[...] = jnp.zeros_like(l_ref)
        a_ref[...] = jnp.zeros_like(a_ref)

    def block(st, width, masked):
        kb = k_ref[0, pl.ds(st, width), :]
        vb = v_ref[0, pl.ds(st, width), :]
        for qm, m_ref, l_ref, a_ref in stats:
            s = _dot_nt(qm, kb)
            if masked:
                row = lax.broadcasted_iota(jnp.int32, (tq, width), 0)
                col = lax.broadcasted_iota(jnp.int32, (tq, width), 1)
                s = jnp.where(col <= row + (width - tq), s, NEG_BIG)
            m_old = m_ref[...]
            m_new = jnp.maximum(m_old, jnp.max(s, axis=-1, keepdims=True))
            alpha = jnp.exp2(m_old - m_new)
            p = jnp.concatenate([jnp.exp2(s[:, c * HEAD_W:(c + 1) * HEAD_W] - m_new)
                                 for c in range(width // HEAD_W)], axis=-1)
            l_new = alpha * l_ref[...]
            for c in range(width // HEAD_W):
                l_new = l_new + p[:, c * HEAD_W:(c + 1) * HEAD_W]
            m_ref[...] = m_new
            l_ref[...] = l_new
            a_ref[...] = alpha * a_ref[...] + _dot(p.astype(BF16), vb)

    def body(j, carry):
        block(pl.multiple_of(j * tk, tk), tk, False)
        return carry

    lax.fori_loop(0, (i * tq) // tk, body, 0)

    @pl.when(i % 2 == 1)
    def _():
        block(pl.multiple_of((i - 1) * tq, tk), tk, True)

    @pl.when(i % 2 == 0)
    def _():
        block(pl.multiple_of(i * tq, tk), tq, True)

    lp = lam_ref[...]
    lam = (jnp.exp(jnp.sum(lp[0:1] * lp[1:2], axis=-1, keepdims=True))
           - jnp.exp(jnp.sum(lp[2:3] * lp[3:4], axis=-1, keepdims=True)) + LAMBDA_INIT)
    r1 = 1.0 / jnp.sum(l1_ref[...], axis=-1, keepdims=True)
    r2 = 1.0 / jnp.sum(l2_ref[...], axis=-1, keepdims=True)
    o = a1_ref[...] * r1 - lam * (a2_ref[...] * r2)
    o_ref[0] = (_rms(o, sw_ref[...]) * (1.0 - LAMBDA_INIT)).astype(BF16)


def _attention(q, k, v, lam_p, subln_w, tq):
    b, t, _ = q.shape
    tk = 2 * tq
    assert t % tq == 0
    grid = (b, DA_HEADS, t // tq)
    return pl.pallas_call(
        functools.partial(_attn_kernel, tq=tq, tk=tk),
        grid=grid,
        in_specs=[pl.BlockSpec((1, tq, HEAD_W), lambda bi, h, i: (bi, i, h)),
                  pl.BlockSpec((1, t, HEAD_W), lambda bi, h, i: (bi, 0, h)),
                  pl.BlockSpec((1, t, HEAD_W), lambda bi, h, i: (bi, 0, h)),
                  pl.BlockSpec(lam_p.shape, lambda bi, h, i: (0, 0)),
                  pl.BlockSpec((1, HEAD_W), lambda bi, h, i: (0, 0))],
        out_specs=pl.BlockSpec((1, tq, HEAD_W), lambda bi, h, i: (bi, i, h)),
        out_shape=jax.ShapeDtypeStruct((b, t, GROUP_W), BF16),
        scratch_shapes=[pltpu.VMEM((tq, HEAD_W), F32)] * 6,
        compiler_params=pltpu.CompilerParams(
            dimension_semantics=("parallel", "parallel", "arbitrary"), vmem_limit_bytes=VMEM_LIMIT),
        name="diff_attn",
    )(q, k, v, lam_p, subln_w)


def _gdn_kernel(gq_ref, gk_ref, gv_ref, gz_ref, gba_ref, cw_ref, alog_ref, dtb_ref, nw_ref, o_ref,
                xe_ref, s_ref, u_ref, w_ref, qk_ref, qd_ref, kdt_ref, gl_ref, *, tt, c):
    nh = GDN_HEADS
    nc = tt // c
    ti = pl.program_id(1)

    @pl.when(ti == 0)
    def _():
        xe_ref[0:8, :] = jnp.zeros((8, 3 * GROUP_W), F32)
        s_ref[...] = jnp.zeros_like(s_ref)

    xe_ref[8:8 + tt, 0:GROUP_W] = gq_ref[0].astype(F32)
    xe_ref[8:8 + tt, GROUP_W:2 * GROUP_W] = gk_ref[0].astype(F32)
    xe_ref[8:8 + tt, 2 * GROUP_W:3 * GROUP_W] = gv_ref[0].astype(F32)
    cw = cw_ref[...]
    y = cw[SHORT_CONV - 1:SHORT_CONV] * xe_ref[8:8 + tt, :]
    for j in range(1, SHORT_CONV):
        y = y + cw[SHORT_CONV - 1 - j:SHORT_CONV - j] * xe_ref[8 - j:8 - j + tt, :]
    xe_ref[0:8, :] = xe_ref[tt:tt + 8, :]
    y = y * _sigmoid(y)

    gba = gba_ref[0]
    beta_all = _sigmoid(gba)
    xa = gba + dtb_ref[...]
    g_all = -jnp.exp(alog_ref[...]) * (jnp.maximum(xa, 0.0) + jnp.log1p(jnp.exp(-jnp.abs(xa))))

    r_i = lax.broadcasted_iota(jnp.int32, (c, c), 0)
    c_i = lax.broadcasted_iota(jnp.int32, (c, c), 1)
    tril = c_i <= r_i
    strict = c_i < r_i
    eye = (c_i == r_i).astype(F32)
    tril_f = tril.astype(F32)
    blk_mask = [(r_i // INV_BLOCK) == (c_i // INV_BLOCK)]
    size = INV_BLOCK
    while size < c:
        blk_mask.append(((r_i // (2 * size)) == (c_i // (2 * size))) & ((r_i // size) > (c_i // size)))
        size *= 2

    def level(fn, items):
        return [fn(x) for x in items]

    for g0 in range(0, nc, GDN_GROUP):
        prep = []
        for ci in range(g0, g0 + GDN_GROUP):
            rows = slice(ci * c, (ci + 1) * c)
            gcum = jnp.dot(tril_f, g_all[rows], preferred_element_type=F32, precision=lax.Precision.HIGHEST)
            gcum_t = gcum.T
            for h in range(nh):
                q = y[rows, h * HEAD_W:(h + 1) * HEAD_W]
                k = y[rows, GROUP_W + h * HEAD_W: GROUP_W + (h + 1) * HEAD_W]
                v = y[rows, 2 * GROUP_W + h * HEAD_W: 2 * GROUP_W + (h + 1) * HEAD_W]
                q = q * lax.rsqrt(jnp.sum(q * q, axis=-1, keepdims=True) + EPS) * (HEAD_W ** -0.5)
                k = k * lax.rsqrt(jnp.sum(k * k, axis=-1, keepdims=True) + EPS)
                beta = beta_all[rows, h:h + 1]
                gc = gcum[:, nh + h:nh + h + 1]
                gr = gcum_t[nh + h:nh + h + 1, :]
                g_last = gcum[c - 1:c, nh + h:nh + h + 1]
                decay = jnp.exp(jnp.where(tril, gc - gr, -jnp.inf))
                e_gc = jnp.exp(gc)
                kb = k * beta
                idx = ci * nh + h
                qd_ref[idx] = (q * e_gc).astype(BF16)
                kdt_ref[idx] = (k * jnp.exp(g_last - gc)).T.astype(BF16)
                gl_ref[idx] = jnp.broadcast_to(jnp.exp(g_last), (8, HEAD_W))
                rhs = jnp.concatenate([v * beta, kb * e_gc], axis=-1).astype(BF16)
                prep.append((idx, q.astype(BF16), k.astype(BF16), kb.astype(BF16), decay, rhs))
        kkt = level(lambda a: _dot_nt(a[3], a[2]) * a[4], prep)
        for a, qk in zip(prep, level(lambda a: _dot_nt(a[1], a[2]) * a[4], prep)):
            qk_ref[a[0]] = qk.astype(BF16)
        lmat = [jnp.where(strict, x, 0.0) for x in kkt]
        p = [jnp.where(blk_mask[0], -x, 0.0) for x in lmat]
        t_inv = [eye + x for x in p]
        for _ in range(int(math.log2(INV_BLOCK)) - 1):
            p = level(lambda x: _dot(x.astype(BF16), x.astype(BF16)), p)
            t_inv = level(lambda tx: tx[0] + _dot(tx[0].astype(BF16), tx[1].astype(BF16)), list(zip(t_inv, p)))
        for off_mask in blk_mask[1:]:
            t_bf = [x.astype(BF16) for x in t_inv]
            x_off = level(lambda tl: _dot(tl[0], jnp.where(off_mask, tl[1], 0.0).astype(BF16)).astype(BF16),
                          list(zip(t_bf, lmat)))
            t_inv = level(lambda txb: txb[0] - _dot(txb[1], txb[2]), list(zip(t_inv, x_off, t_bf)))
        for a, sol in zip(prep, level(lambda ta: _dot(ta[0].astype(BF16), ta[1][5]), list(zip(t_inv, prep)))):
            u_ref[a[0]] = sol[:, :HEAD_W]
            w_ref[a[0]] = sol[:, HEAD_W:].astype(BF16)

    nw = nw_ref[...]
    heads = range(nh)
    for ci in range(nc):
        rows = slice(ci * c, (ci + 1) * c)
        s_old = [s_ref[h] for h in heads]
        s_bf = [x.astype(BF16) for x in s_old]
        ws = [_dot(w_ref[ci * nh + h], s_bf[h]) for h in heads]
        qs = [_dot(qd_ref[ci * nh + h], s_bf[h]) for h in heads]
        v_bf = [(u_ref[ci * nh + h] - ws[h]).astype(BF16) for h in heads]
        kv = [_dot(kdt_ref[ci * nh + h], v_bf[h]) for h in heads]
        for h in heads:
            s_ref[h] = s_old[h] * gl_ref[ci * nh + h][0:1, :] + kv[h]
        o = [qs[h] + _dot(qk_ref[ci * nh + h], v_bf[h]) for h in heads]
        for h in heads:
            cols = slice(h * HEAD_W, (h + 1) * HEAD_W)
            z = gz_ref[0, rows, cols].astype(F32)
            o_ref[0, rows, cols] = (_rms(o[h], nw) * (z * _sigmoid(z))).astype(BF16)


def _gdn(gq, gk, gv, gz, gba, conv_w, alog_row, dtb_row, norm_w, tt, c):
    b, t, _ = gq.shape
    nidx = (tt // c) * GDN_HEADS
    tok_spec = lambda width: pl.BlockSpec((1, tt, width), lambda i, j: (i, j, 0))
    const = lambda shape: pl.BlockSpec(shape, lambda i, j: (0,) * len(shape))
    return pl.pallas_call(
        functools.partial(_gdn_kernel, tt=tt, c=c),
        grid=(b, t // tt),
        in_specs=[tok_spec(GROUP_W)] * 4 + [tok_spec(HEAD_W), const(conv_w.shape),
                                            const((1, HEAD_W)), const((1, HEAD_W)), const((1, HEAD_W))],
        out_specs=tok_spec(GROUP_W),
        out_shape=jax.ShapeDtypeStruct((b, t, GROUP_W), BF16),
        scratch_shapes=[pltpu.VMEM((8 + tt, 3 * GROUP_W), F32),
                        pltpu.VMEM((GDN_HEADS, HEAD_W, HEAD_W), F32),
                        pltpu.VMEM((nidx, c, HEAD_W), F32),
                        pltpu.VMEM((nidx, c, HEAD_W), BF16),
                        pltpu.VMEM((nidx, c, c), BF16),
                        pltpu.VMEM((nidx, c, HEAD_W), BF16),
                        pltpu.VMEM((nidx, HEAD_W, c), BF16),
                        pltpu.VMEM((nidx, 8, HEAD_W), F32)],
        compiler_params=pltpu.CompilerParams(
            dimension_semantics=("parallel", "arbitrary"), vmem_limit_bytes=VMEM_LIMIT),
        name="gdn",
    )(gq, gk, gv, gz, gba, conv_w, alog_row, dtb_row, norm_w)


def _ffn_kernel(x_ref, yda_ref, ygdn_ref, wout_ref, fnw_ref, wup_ref, cw_ref, wdown_ref, finw_ref, o_ref,
                hx_ref, carry_ref, *, tm):
    n_chunks = D_FF // FF_CHUNK
    cw2 = 2 * FF_CHUNK

    @pl.when(pl.program_id(1) == 0)
    def _():
        carry_ref[...] = jnp.zeros_like(carry_ref)

    h = (x_ref[0] + _dot(yda_ref[0], wout_ref[0:GROUP_W, :])
         + _dot(ygdn_ref[0], wout_ref[GROUP_W:2 * GROUP_W, :]))
    hn = _rms(h, fnw_ref[...]).astype(BF16)
    acc = jnp.zeros((tm, D_MODEL), F32)
    for ci in range(n_chunks):
        cols = slice(ci * cw2, (ci + 1) * cw2)
        hx_ref[0:8, :] = carry_ref[:, cols]
        hx_ref[8:8 + tm, :] = _dot(hn, wup_ref[:, cols])
        carry_ref[:, cols] = hx_ref[tm:tm + 8, :]
        cw = cw_ref[:, cols]
        y = cw[FFN_CONV - 1:FFN_CONV] * hx_ref[8:8 + tm, :]
        for j in range(1, FFN_CONV):
            y = y + cw[FFN_CONV - 1 - j:FFN_CONV - j] * hx_ref[8 - j:8 - j + tm, :]
        gate = y[:, :FF_CHUNK]
        act = (gate * _sigmoid(gate) * y[:, FF_CHUNK:]).astype(BF16)
        acc = acc + _dot(act, wdown_ref[ci * FF_CHUNK:(ci + 1) * FF_CHUNK, :])
    o_ref[0] = _rms(h + acc, finw_ref[...])


def _ffn(x, y_da, y_gdn, w_out, ffn_norm_w, w_up, conv_w, w_down, final_norm_w, tm):
    b, t, d = x.shape
    tok_spec = lambda width: pl.BlockSpec((1, tm, width), lambda i, j: (i, j, 0))
    const = lambda shape: pl.BlockSpec(shape, lambda i, j: (0,) * len(shape),
                                       pipeline_mode=pl.Buffered(1))
    return pl.pallas_call(
        functools.partial(_ffn_kernel, tm=tm),
        grid=(b, t // tm),
        in_specs=[tok_spec(d), tok_spec(GROUP_W), tok_spec(GROUP_W), const(w_out.shape), const((1, d)),
                  const(w_up.shape), const(conv_w.shape), const(w_down.shape), const((1, d))],
        out_specs=tok_spec(d),
        out_shape=jax.ShapeDtypeStruct((b, t, d), F32),
        scratch_shapes=[pltpu.VMEM((8 + tm, 2 * FF_CHUNK), F32),
                        pltpu.VMEM((8, 2 * D_FF), F32)],
        compiler_params=pltpu.CompilerParams(
            dimension_semantics=("parallel", "arbitrary"), vmem_limit_bytes=VMEM_LIMIT),
        name="outproj_ffn",
    )(x, y_da, y_gdn, w_out, ffn_norm_w, w_up, conv_w, w_down, final_norm_w)


def _qk_col_perm():
    half = DA_HEAD_DIM // 2
    perm = []
    for h in range(DA_HEADS):
        for hi in range(2):
            for cmap in range(2):
                base = h * HEAD_W + cmap * DA_HEAD_DIM + hi * half
                perm.extend(range(base, base + half))
    return np.asarray(perm, dtype=np.int32)


def _ffn_col_perm():
    perm = []
    for ci in range(D_FF // FF_CHUNK):
        perm.extend(range(ci * FF_CHUNK, (ci + 1) * FF_CHUNK))
        perm.extend(range(D_FF + ci * FF_CHUNK, D_FF + (ci + 1) * FF_CHUNK))
    return np.asarray(perm, dtype=np.int32)


def _rope_tables(t):
    half = DA_HEAD_DIM // 2
    inv_freq = ROPE_THETA ** (-jnp.arange(0, DA_HEAD_DIM, 2, dtype=F32) / DA_HEAD_DIM)
    ang = jnp.arange(t, dtype=F32)[:, None] * inv_freq[None, :]
    cos, sin = jnp.cos(ang), jnp.sin(ang)
    assert cos.shape == (t, half)
    return jnp.tile(cos, (1, 4)), jnp.concatenate([-sin, -sin, sin, sin], axis=-1)


def _tile(t, pref):
    while t % pref:
        pref //= 2
    return pref


def kernel(x, attn_norm_w, w_in, da_lambda_q1, da_lambda_k1, da_lambda_q2, da_lambda_k2, da_subln_w,
           gdn_conv_w, gdn_a_log, gdn_dt_bias, gdn_norm_w, w_out, ffn_norm_w, ffn_w_up, ffn_conv_w,
           ffn_w_down, final_norm_w):
    b, t, d = x.shape
    assert d == D_MODEL and t % GDN_CHUNK == 0
    l = 0
    w = w_in[l]
    perm = _qk_col_perm()
    nh = GDN_HEADS
    small = jnp.pad(w[:, 7 * GROUP_W:], ((0, 0), (0, HEAD_W - 2 * nh)))
    w_all = jnp.concatenate([w[:, :GROUP_W][:, perm], w[:, GROUP_W:2 * GROUP_W][:, perm],
                             w[:, 2 * GROUP_W:7 * GROUP_W], small], axis=-1).astype(BF16)
    cos_t, sin_t = _rope_tables(t)

    q, k, v, gq, gk, gv, gz, gba = _inproj(x, attn_norm_w[l][None, :], w_all, cos_t, sin_t, _tile(t, 512))

    lam_p = jnp.stack([da_lambda_q1[l], da_lambda_k1[l], da_lambda_q2[l], da_lambda_k2[l]])
    y_da = _attention(q, k, v, lam_p, da_subln_w[l][None, :], _tile(t, 512))

    lane_pad = lambda vec: jnp.pad(vec, (nh, HEAD_W - 2 * nh))[None, :]
    y_gdn = _gdn(gq, gk, gv, gz, gba, gdn_conv_w[l], lane_pad(gdn_a_log[l]), lane_pad(gdn_dt_bias[l]),
                 gdn_norm_w[l][None, :], _tile(t, 512), GDN_CHUNK)

    fperm = _ffn_col_perm()
    return _ffn(x, y_da, y_gdn, w_out[l].astype(BF16), ffn_norm_w[l][None, :],
                ffn_w_up[l][:, fperm].astype(BF16), ffn_conv_w[l][:, fperm],
                ffn_w_down[l].astype(BF16), final_norm_w[None, :], _tile(t, 512))
```

```python
import functools
import math

import jax
import jax.numpy as jnp
import numpy as np
from jax import lax
from jax.experimental import pallas as pl
from jax.experimental.pallas import tpu as pltpu

F32 = jnp.float32
BF16 = jnp.bfloat16

D_MODEL = 1024
DA_HEADS = 4
DA_HEAD_DIM = 64
HEAD_W = 128
GROUP_W = 512
GDN_HEADS = 4
SHORT_CONV = 4
D_FF = 2816
FFN_CONV = 3
FF_CHUNK = 256
FF_WINDOWS = 4
ROPE_THETA = 10000.0
EPS = 1e-6
LAMBDA_INIT = 0.8 - 0.6 * math.exp(-0.3 * 0)
GDN_CHUNK = 128
INV_BLOCK = 16
GDN_GROUP = 2
NEG_BIG = -1e30
LOG2E = math.log2(math.e)
VMEM_LIMIT = 56 * 1024 * 1024


def _rms(x, w):
    return x * lax.rsqrt(jnp.mean(x * x, axis=-1, keepdims=True) + EPS) * w


def _dot(a, b):
    return jnp.dot(a, b, preferred_element_type=F32)


def _dot_nt(a, b):
    return lax.dot_general(a, b, (((1,), (1,)), ((), ())), preferred_element_type=F32)


def _sigmoid(x):
    return 1.0 / (1.0 + jnp.exp(-x))


def _inproj_kernel(x_ref, nw_ref, w_ref, cos_ref, sin_ref,
                   q_ref, k_ref, v_ref, gq_ref, gk_ref, gv_ref, gz_ref, gba_ref):
    xn = _rms(x_ref[0], nw_ref[...]).astype(BF16)
    cos = cos_ref[...]
    sin = sin_ref[...]

    def proj(seg, width=GROUP_W):
        return _dot(xn, w_ref[:, seg * GROUP_W: seg * GROUP_W + width])

    def rope(p, scale):
        outs = []
        for h in range(DA_HEADS):
            ph = p[:, h * HEAD_W:(h + 1) * HEAD_W]
            outs.append((ph * cos + pltpu.roll(ph, HEAD_W // 2, 1) * sin) * scale)
        return jnp.concatenate(outs, axis=-1)

    q_ref[0] = rope(proj(0), DA_HEAD_DIM ** -0.5 * LOG2E).astype(BF16)
    k_ref[0] = rope(proj(1), 1.0).astype(BF16)
    v_ref[0] = proj(2).astype(BF16)
    gq_ref[0] = proj(3).astype(BF16)
    gk_ref[0] = proj(4).astype(BF16)
    gv_ref[0] = proj(5).astype(BF16)
    gz_ref[0] = proj(6).astype(BF16)
    gba_ref[0] = proj(7, HEAD_W)


def _inproj(x, norm_w, w_all, cos_t, sin_t, tm):
    b, t, d = x.shape
    grid = (b, t // tm)
    tok = lambda width, dt: jax.ShapeDtypeStruct((b, t, width), dt)
    tok_spec = lambda width: pl.BlockSpec((1, tm, width), lambda i, j: (i, j, 0))
    const = lambda shape: pl.BlockSpec(shape, lambda i, j: (0,) * len(shape),
                                       pipeline_mode=pl.Buffered(1))
    return pl.pallas_call(
        _inproj_kernel,
        grid=grid,
        in_specs=[tok_spec(d), const((1, d)), const(w_all.shape),
                  pl.BlockSpec((tm, HEAD_W), lambda i, j: (j, 0)),
                  pl.BlockSpec((tm, HEAD_W), lambda i, j: (j, 0))],
        out_specs=[tok_spec(GROUP_W)] * 7 + [tok_spec(HEAD_W)],
        out_shape=[tok(GROUP_W, BF16)] * 7 + [tok(HEAD_W, F32)],
        compiler_params=pltpu.CompilerParams(
            dimension_semantics=("parallel", "parallel"), vmem_limit_bytes=VMEM_LIMIT),
        name="inproj",
    )(x, norm_w, w_all, cos_t, sin_t)


def _attn_kernel(q_ref, k_ref, v_ref, lam_ref, sw_ref, o_ref, m_ref, l_ref, a_ref, *, tq, tk):
    i = pl.program_id(1)
    lane = lax.broadcasted_iota(jnp.int32, (1, HEAD_W), 1)
    first_map = (lane % 64) < 32
    head_cols = [slice(h * HEAD_W, (h + 1) * HEAD_W) for h in range(DA_HEADS)]
    streams = []
    for h in range(DA_HEADS):
        q = q_ref[0, :, head_cols[h]]
        zero = jnp.zeros_like(q)
        streams.append((h, 2 * h, jnp.where(first_map, q, zero)))
        streams.append((h, 2 * h + 1, jnp.where(first_map, zero, q)))
    m_ref[...] = jnp.full_like(m_ref, NEG_BIG)
    l_ref[...] = jnp.zeros_like(l_ref)
    a_ref[...] = jnp.zeros_like(a_ref)

    def block(st, width, masked):
        kb = [k_ref[0, pl.ds(st, width), head_cols[h]] for h in range(DA_HEADS)]
        vb = [v_ref[0, pl.ds(st, width), head_cols[h]] for h in range(DA_HEADS)]
        scores = [_dot_nt(qm, kb[h]) for h, _, qm in streams]
        for s, (h, si, _) in zip(scores, streams):
            if masked:
                row = lax.broadcasted_iota(jnp.int32, (tq, width), 0)
                col = lax.broadcasted_iota(jnp.int32, (tq, width), 1)
                s = jnp.where(col <= row + (width - tq), s, NEG_BIG)
            m_old = m_ref[si]
            m_new = jnp.maximum(m_old, jnp.max(s, axis=-1, keepdims=True))
            alpha = jnp.exp2(m_old - m_new)
            p = jnp.concatenate([jnp.exp2(s[:, c * HEAD_W:(c + 1) * HEAD_W] - m_new)
                                 for c in range(width // HEAD_W)], axis=-1)
            l_new = alpha * l_ref[si]
            for c in range(width // HEAD_W):
                l_new = l_new + p[:, c * HEAD_W:(c + 1) * HEAD_W]
            m_ref[si] = m_new
            l_ref[si] = l_new
            a_ref[si] = alpha * a_ref[si] + _dot(p.astype(BF16), vb[h])

    def body(j, carry):
        block(pl.multiple_of(j * tk, tk), tk, False)
        return carry

    lax.fori_loop(0, (i * tq) // tk, body, 0)

    @pl.when(i % 2 == 1)
    def _():
        block(pl.multiple_of((i - 1) * tq, tk), tk, True)

    @pl.when(i % 2 == 0)
    def _():
        block(pl.multiple_of(i * tq, tk), tq, True)

    lp = lam_ref[...]
    lam = (jnp.exp(jnp.sum(lp[0:1] * lp[1:2], axis=-1, keepdims=True))
           - jnp.exp(jnp.sum(lp[2:3] * lp[3:4], axis=-1, keepdims=True)) + LAMBDA_INIT)
    for h in range(DA_HEADS):
        r1 = 1.0 / jnp.sum(l_ref[2 * h], axis=-1, keepdims=True)
        r2 = 1.0 / jnp.sum(l_ref[2 * h + 1], axis=-1, keepdims=True)
        o = a_ref[2 * h] * r1 - lam * (a_ref[2 * h + 1] * r2)
        o_ref[0, :, head_cols[h]] = (_rms(o, sw_ref[...]) * (1.0 - LAMBDA_INIT)).astype(BF16)


def _attention(q, k, v, lam_p, subln_w, tq):
    b, t, _ = q.shape
    tk = 2 * tq
    assert t % tq == 0
    n_streams = 2 * DA_HEADS
    return pl.pallas_call(
        functools.partial(_attn_kernel, tq=tq, tk=tk),
        grid=(b, t // tq),
        in_specs=[pl.BlockSpec((1, tq, GROUP_W), lambda bi, i: (bi, i, 0)),
                  pl.BlockSpec((1, t, GROUP_W), lambda bi, i: (bi, 0, 0)),
                  pl.BlockSpec((1, t, GROUP_W), lambda bi, i: (bi, 0, 0)),
                  pl.BlockSpec(lam_p.shape, lambda bi, i: (0, 0)),
                  pl.BlockSpec((1, HEAD_W), lambda bi, i: (0, 0))],
        out_specs=pl.BlockSpec((1, tq, GROUP_W), lambda bi, i: (bi, i, 0)),
        out_shape=jax.ShapeDtypeStruct((b, t, GROUP_W), BF16),
        scratch_shapes=[pltpu.VMEM((n_streams, tq, HEAD_W), F32)] * 3,
        compiler_params=pltpu.CompilerParams(
            dimension_semantics=("parallel", "arbitrary"), vmem_limit_bytes=VMEM_LIMIT),
        name="diff_attn",
    )(q, k, v, lam_p, subln_w)


def _gdn_kernel(gq_ref, gk_ref, gv_ref, gz_ref, gba_ref, cw_ref, alog_ref, dtb_ref, nw_ref, o_ref,
                xe_ref, s_ref, u_ref, w_ref, qk_ref, qd_ref, kdt_ref, gl_ref, *, tt, c):
    nh = GDN_HEADS
    nc = tt // c
    ti = pl.program_id(1)

    @pl.when(ti == 0)
    def _():
        xe_ref[0:8, :] = jnp.zeros((8, 3 * GROUP_W), F32)
        s_ref[...] = jnp.zeros_like(s_ref)

    xe_ref[8:8 + tt, 0:GROUP_W] = gq_ref[0].astype(F32)
    xe_ref[8:8 + tt, GROUP_W:2 * GROUP_W] = gk_ref[0].astype(F32)
    xe_ref[8:8 + tt, 2 * GROUP_W:3 * GROUP_W] = gv_ref[0].astype(F32)
    cw = cw_ref[...]
    y = cw[SHORT_CONV - 1:SHORT_CONV] * xe_ref[8:8 + tt, :]
    for j in range(1, SHORT_CONV):
        y = y + cw[SHORT_CONV - 1 - j:SHORT_CONV - j] * xe_ref[8 - j:8 - j + tt, :]
    xe_ref[0:8, :] = xe_ref[tt:tt + 8, :]
    y = y * _sigmoid(y)

    gba = gba_ref[0]
    beta_all = _sigmoid(gba)
    xa = gba + dtb_ref[...]
    g_all = -jnp.exp(alog_ref[...]) * (jnp.maximum(xa, 0.0) + jnp.log1p(jnp.exp(-jnp.abs(xa))))

    r_i = lax.broadcasted_iota(jnp.int32, (c, c), 0)
    c_i = lax.broadcasted_iota(jnp.int32, (c, c), 1)
    tril = c_i <= r_i
    strict = c_i < r_i
    eye = (c_i == r_i).astype(F32)
    tril_f = tril.astype(F32)
    blk_mask = [(r_i // INV_BLOCK) == (c_i // INV_BLOCK)]
    size = INV_BLOCK
    while size < c:
        blk_mask.append(((r_i // (2 * size)) == (c_i // (2 * size))) & ((r_i // size) > (c_i // size)))
        size *= 2

    def level(fn, items):
        return [fn(x) for x in items]

    for g0 in range(0, nc, GDN_GROUP):
        prep = []
        for ci in range(g0, g0 + GDN_GROUP):
            rows = slice(ci * c, (ci + 1) * c)
            gcum = jnp.dot(tril_f, g_all[rows], preferred_element_type=F32, precision=lax.Precision.HIGHEST)
            gcum_t = gcum.T
            for h in range(nh):
                q = y[rows, h * HEAD_W:(h + 1) * HEAD_W]
                k = y[rows, GROUP_W + h * HEAD_W: GROUP_W + (h + 1) * HEAD_W]
                v = y[rows, 2 * GROUP_W + h * HEAD_W: 2 * GROUP_W + (h + 1) * HEAD_W]
                q = q * lax.rsqrt(jnp.sum(q * q, axis=-1, keepdims=True) + EPS) * (HEAD_W ** -0.5)
                k = k * lax.rsqrt(jnp.sum(k * k, axis=-1, keepdims=True) + EPS)
                beta = beta_all[rows, h:h + 1]
                gc = gcum[:, nh + h:nh + h + 1]
                gr = gcum_t[nh + h:nh + h + 1, :]
                g_last = gcum[c - 1:c, nh + h:nh + h + 1]
                decay = jnp.exp(jnp.where(tril, gc - gr, -jnp.inf))
                e_gc = jnp.exp(gc)
                kb = k * beta
                idx = ci * nh + h
                qd_ref[idx] = (q * e_gc).astype(BF16)
                kdt_ref[idx] = (k * jnp.exp(g_last - gc)).T.astype(BF16)
                gl_ref[idx] = jnp.broadcast_to(jnp.exp(g_last), (8, HEAD_W))
                rhs = jnp.concatenate([v * beta, kb * e_gc], axis=-1).astype(BF16)
                prep.append((idx, q.astype(BF16), k.astype(BF16), kb.astype(BF16), decay, rhs))
        kkt = level(lambda a: _dot_nt(a[3], a[2]) * a[4], prep)
        for a, qk in zip(prep, level(lambda a: _dot_nt(a[1], a[2]) * a[4], prep)):
            qk_ref[a[0]] = qk.astype(BF16)
        lmat = [jnp.where(strict, x, 0.0) for x in kkt]
        p = [jnp.where(blk_mask[0], -x, 0.0) for x in lmat]
        t_inv = [eye + x for x in p]
        for _ in range(int(math.log2(INV_BLOCK)) - 1):
            p = level(lambda x: _dot(x.astype(BF16), x.astype(BF16)), p)
            t_inv = level(lambda tx: tx[0] + _dot(tx[0].astype(BF16), tx[1].astype(BF16)), list(zip(t_inv, p)))
        for off_mask in blk_mask[1:]:
            t_bf = [x.astype(BF16) for x in t_inv]
            x_off = level(lambda tl: _dot(tl[0], jnp.where(off_mask, tl[1], 0.0).astype(BF16)).astype(BF16),
                          list(zip(t_bf, lmat)))
            t_inv = level(lambda txb: txb[0] - _dot(txb[1], txb[2]), list(zip(t_inv, x_off, t_bf)))
        for a, sol in zip(prep, level(lambda ta: _dot(ta[0].astype(BF16), ta[1][5]), list(zip(t_inv, prep)))):
            u_ref[a[0]] = sol[:, :HEAD_W]
            w_ref[a[0]] = sol[:, HEAD_W:].astype(BF16)

    nw = nw_ref[...]
    heads = range(nh)
    for ci in range(nc):
        rows = slice(ci * c, (ci + 1) * c)
        s_old = [s_ref[h] for h in heads]
        s_bf = [x.astype(BF16) for x in s_old]
        ws = [_dot(w_ref[ci * nh + h], s_bf[h]) for h in heads]
        qs = [_dot(qd_ref[ci * nh + h], s_bf[h]) for h in heads]
        v_bf = [(u_ref[ci * nh + h] - ws[h]).astype(BF16) for h in heads]
        kv = [_dot(kdt_ref[ci * nh + h], v_bf[h]) for h in heads]
        for h in heads:
            s_ref[h] = s_old[h] * gl_ref[ci * nh + h][0:1, :] + kv[h]
        o = [qs[h] + _dot(qk_ref[ci * nh + h], v_bf[h]) for h in heads]
        for h in heads:
            cols = slice(h * HEAD_W, (h + 1) * HEAD_W)
            z = gz_ref[0, rows, cols].astype(F32)
            o_ref[0, rows, cols] = (_rms(o[h], nw) * (z * _sigmoid(z))).astype(BF16)


def _gdn(gq, gk, gv, gz, gba, conv_w, alog_row, dtb_row, norm_w, tt, c):
    b, t, _ = gq.shape
    nidx = (tt // c) * GDN_HEADS
    tok_spec = lambda width: pl.BlockSpec((1, tt, width), lambda i, j: (i, j, 0))
    const = lambda shape: pl.BlockSpec(shape, lambda i, j: (0,) * len(shape))
    return pl.pallas_call(
        functools.partial(_gdn_kernel, tt=tt, c=c),
        grid=(b, t // tt),
        in_specs=[tok_spec(GROUP_W)] * 4 + [tok_spec(HEAD_W), const(conv_w.shape),
                                            const((1, HEAD_W)), const((1, HEAD_W)), const((1, HEAD_W))],
        out_specs=tok_spec(GROUP_W),
        out_shape=jax.ShapeDtypeStruct((b, t, GROUP_W), BF16),
        scratch_shapes=[pltpu.VMEM((8 + tt, 3 * GROUP_W), F32),
                        pltpu.VMEM((GDN_HEADS, HEAD_W, HEAD_W), F32),
                        pltpu.VMEM((nidx, c, HEAD_W), F32),
                        pltpu.VMEM((nidx, c, HEAD_W), BF16),
                        pltpu.VMEM((nidx, c, c), BF16),
                        pltpu.VMEM((nidx, c, HEAD_W), BF16),
                        pltpu.VMEM((nidx, HEAD_W, c), BF16),
                        pltpu.VMEM((nidx, 8, HEAD_W), F32)],
        compiler_params=pltpu.CompilerParams(
            dimension_semantics=("parallel", "arbitrary"), vmem_limit_bytes=VMEM_LIMIT),
        name="gdn",
    )(gq, gk, gv, gz, gba, conv_w, alog_row, dtb_row, norm_w)


def _ffn_kernel(x_ref, yda_ref, ygdn_ref, wout_ref, fnw_ref, wup_ref, cw_ref, wdown_ref, finw_ref, o_ref,
                hx_refs, carry_ref, *, tm):
    n_chunks = D_FF // FF_CHUNK
    cw2 = 2 * FF_CHUNK

    @pl.when(pl.program_id(1) == 0)
    def _():
        carry_ref[...] = jnp.zeros_like(carry_ref)

    h = (x_ref[0] + _dot(yda_ref[0], wout_ref[0:GROUP_W, :])
         + _dot(ygdn_ref[0], wout_ref[GROUP_W:2 * GROUP_W, :]))
    hn = _rms(h, fnw_ref[...]).astype(BF16)
    acc = jnp.zeros((tm, D_MODEL), F32)

    def up_proj(ci):
        cols = slice(ci * cw2, (ci + 1) * cw2)
        hx_ref = hx_refs.at[ci % FF_WINDOWS]
        hx_ref[0:8, :] = carry_ref[:, cols]
        hx_ref[8:8 + tm, :] = _dot(hn, wup_ref[:, cols])
        carry_ref[:, cols] = hx_ref[tm:tm + 8, :]

    for ci in range(FF_WINDOWS - 1):
        up_proj(ci)
    for ci in range(n_chunks):
        if ci + FF_WINDOWS - 1 < n_chunks:
            up_proj(ci + FF_WINDOWS - 1)
        hx_ref = hx_refs.at[ci % FF_WINDOWS]
        cw = cw_ref[:, ci * cw2:(ci + 1) * cw2]
        y = cw[FFN_CONV - 1:FFN_CONV] * hx_ref[8:8 + tm, :]
        for j in range(1, FFN_CONV):
            y = y + cw[FFN_CONV - 1 - j:FFN_CONV - j] * hx_ref[8 - j:8 - j + tm, :]
        hg = 0.5 * y[:, :FF_CHUNK]
        act = (hg * (1.0 + jnp.tanh(hg)) * y[:, FF_CHUNK:]).astype(BF16)
        acc = acc + _dot(act, wdown_ref[ci * FF_CHUNK:(ci + 1) * FF_CHUNK, :])
    o_ref[0] = _rms(h + acc, finw_ref[...])


def _ffn(x, y_da, y_gdn, w_out, ffn_norm_w, w_up, conv_w, w_down, final_norm_w, tm):
    b, t, d = x.shape
    tok_spec = lambda width: pl.BlockSpec((1, tm, width), lambda i, j: (i, j, 0))
    const = lambda shape: pl.BlockSpec(shape, lambda i, j: (0,) * len(shape),
                                       pipeline_mode=pl.Buffered(1))
    return pl.pallas_call(
        functools.partial(_ffn_kernel, tm=tm),
        grid=(b, t // tm),
        in_specs=[tok_spec(d), tok_spec(GROUP_W), tok_spec(GROUP_W), const(w_out.shape), const((1, d)),
                  const(w_up.shape), const(conv_w.shape), const(w_down.shape), const((1, d))],
        out_specs=tok_spec(d),
        out_shape=jax.ShapeDtypeStruct((b, t, d), F32),
        scratch_shapes=[pltpu.VMEM((FF_WINDOWS, 8 + tm, 2 * FF_CHUNK), F32),
                        pltpu.VMEM((8, 2 * D_FF), F32)],
        compiler_params=pltpu.CompilerParams(
            dimension_semantics=("parallel", "arbitrary"), vmem_limit_bytes=VMEM_LIMIT),
        name="outproj_ffn",
    )(x, y_da, y_gdn, w_out, ffn_norm_w, w_up, conv_w, w_down, final_norm_w)


def _qk_col_perm():
    half = DA_HEAD_DIM // 2
    perm = []
    for h in range(DA_HEADS):
        for hi in range(2):
            for cmap in range(2):
                base = h * HEAD_W + cmap * DA_HEAD_DIM + hi * half
                perm.extend(range(base, base + half))
    return np.asarray(perm, dtype=np.int32)


def _ffn_col_perm():
    perm = []
    for ci in range(D_FF // FF_CHUNK):
        perm.extend(range(ci * FF_CHUNK, (ci + 1) * FF_CHUNK))
        perm.extend(range(D_FF + ci * FF_CHUNK, D_FF + (ci + 1) * FF_CHUNK))
    return np.asarray(perm, dtype=np.int32)


def _rope_tables(t):
    half = DA_HEAD_DIM // 2
    inv_freq = ROPE_THETA ** (-jnp.arange(0, DA_HEAD_DIM, 2, dtype=F32) / DA_HEAD_DIM)
    ang = jnp.arange(t, dtype=F32)[:, None] * inv_freq[None, :]
    cos, sin = jnp.cos(ang), jnp.sin(ang)
    assert cos.shape == (t, half)
    return jnp.tile(cos, (1, 4)), jnp.concatenate([-sin, -sin, sin, sin], axis=-1)


def _tile(t, pref):
    while t % pref:
        pref //= 2
    return pref


def kernel(x, attn_norm_w, w_in, da_lambda_q1, da_lambda_k1, da_lambda_q2, da_lambda_k2, da_subln_w,
           gdn_conv_w, gdn_a_log, gdn_dt_bias, gdn_norm_w, w_out, ffn_norm_w, ffn_w_up, ffn_conv_w,
           ffn_w_down, final_norm_w):
    b, t, d = x.shape
    assert d == D_MODEL and t % GDN_CHUNK == 0
    l = 0
    w = w_in[l]
    perm = _qk_col_perm()
    nh = GDN_HEADS
    small = jnp.pad(w[:, 7 * GROUP_W:], ((0, 0), (0, HEAD_W - 2 * nh)))
    w_all = jnp.concatenate([w[:, :GROUP_W][:, perm], w[:, GROUP_W:2 * GROUP_W][:, perm],
                             w[:, 2 * GROUP_W:7 * GROUP_W], small], axis=-1).astype(BF16)
    cos_t, sin_t = _rope_tables(t)

    q, k, v, gq, gk, gv, gz, gba = _inproj(x, attn_norm_w[l][None, :], w_all, cos_t, sin_t, _tile(t, 512))

    lam_p = jnp.stack([da_lambda_q1[l], da_lambda_k1[l], da_lambda_q2[l], da_lambda_k2[l]])
    y_da = _attention(q, k, v, lam_p, da_subln_w[l][None, :], _tile(t, 512))

    lane_pad = lambda vec: jnp.pad(vec, (nh, HEAD_W - 2 * nh))[None, :]
    y_gdn = _gdn(gq, gk, gv, gz, gba, gdn_conv_w[l], lane_pad(gdn_a_log[l]), lane_pad(gdn_dt_bias[l]),
                 gdn_norm_w[l][None, :], _tile(t, 512), GDN_CHUNK)

    fperm = _ffn_col_perm()
    return _ffn(x, y_da, y_gdn, w_out[l].astype(BF16), ffn_norm_w[l][None, :],
                ffn_w_up[l][:, fperm].astype(BF16), ffn_conv_w[l][:, fperm],
                ffn_w_down[l].astype(BF16), final_norm_w[None, :], _tile(t, 512))
```

```python
import functools
import math

import jax
import jax.numpy as jnp
from jax import lax
from jax.experimental import pallas as pl
from jax.experimental.pallas import tpu as pltpu

F32 = jnp.float32
BF16 = jnp.bfloat16

D_MODEL = 1024
DA_HEADS = 4
DA_HEAD_DIM = 64
HEAD_W = 128
GROUP_W = 512
GDN_HEADS = 4
SHORT_CONV = 4
D_FF = 2816
FFN_CONV = 3
FF_CHUNK = 256
FF_WINDOWS = 4
ROPE_THETA = 10000.0
EPS = 1e-6
LAMBDA_INIT = 0.8 - 0.6 * math.exp(-0.3 * 0)
GDN_CHUNK = 128
INV_BLOCK = 16
GDN_GROUP = 4
NEG_BIG = -1e30
LOG2E = math.log2(math.e)
VMEM_LIMIT = 56 * 1024 * 1024


def _rms(x, w):
    return x * lax.rsqrt(jnp.mean(x * x, axis=-1, keepdims=True) + EPS) * w


def _dot(a, b):
    return jnp.dot(a, b, preferred_element_type=F32)


def _dot_nt(a, b):
    return lax.dot_general(a, b, (((1,), (1,)), ((), ())), preferred_element_type=F32)


def _sigmoid(x):
    return 1.0 / (1.0 + jnp.exp(-x))


def _silu(x):
    hx = 0.5 * x
    return hx * (1.0 + jnp.tanh(hx))


def _inproj_kernel(x_ref, nw_ref, w_ref, cos_ref, sin_ref, cw_ref,
                   q_ref, k_ref, v_ref, gq_ref, gk_ref, gv_ref, gz_ref, gba_ref,
                   win_refs, carry_ref, *, tm):
    @pl.when(pl.program_id(1) == 0)
    def _():
        carry_ref[...] = jnp.zeros_like(carry_ref)

    xn = _rms(x_ref[0], nw_ref[...]).astype(BF16)
    cos = cos_ref[...]
    sin = sin_ref[...]

    def proj(seg, width=GROUP_W):
        return _dot(xn, w_ref[:, seg * GROUP_W: seg * GROUP_W + width])

    def rope(p, scale):
        outs = []
        for h in range(DA_HEADS):
            ph = p[:, h * HEAD_W:(h + 1) * HEAD_W]
            outs.append((ph * cos + pltpu.roll(ph, HEAD_W // 2, 1) * sin) * scale)
        return jnp.concatenate(outs, axis=-1)

    q_ref[0] = rope(proj(0), DA_HEAD_DIM ** -0.5 * LOG2E).astype(BF16)
    k_ref[0] = rope(proj(1), 1.0).astype(BF16)
    v_ref[0] = proj(2).astype(BF16)

    def short_conv_silu(seg, slot, out_ref, l2_scale):
        cols = slice(slot * GROUP_W, (slot + 1) * GROUP_W)
        win = win_refs.at[slot]
        win[0:8, :] = carry_ref[:, cols]
        win[8:8 + tm, :] = proj(seg)
        carry_ref[:, cols] = win[tm:tm + 8, :]
        for h in range(GDN_HEADS):
            hc = slice(h * HEAD_W, (h + 1) * HEAD_W)
            cw = cw_ref[:, slot * GROUP_W + h * HEAD_W: slot * GROUP_W + (h + 1) * HEAD_W]
            y = cw[SHORT_CONV - 1:SHORT_CONV] * win[8:8 + tm, hc]
            for j in range(1, SHORT_CONV):
                y = y + cw[SHORT_CONV - 1 - j:SHORT_CONV - j] * win[8 - j:8 - j + tm, hc]
            y = _silu(y)
            if l2_scale is not None:
                y = y * (lax.rsqrt(jnp.sum(y * y, axis=-1, keepdims=True) + EPS) * l2_scale)
            out_ref[0, :, hc] = y.astype(BF16)

    short_conv_silu(3, 0, gq_ref, HEAD_W ** -0.5)
    short_conv_silu(4, 1, gk_ref, 1.0)
    short_conv_silu(5, 2, gv_ref, None)
    gz_ref[0] = proj(6).astype(BF16)
    gba_ref[0] = proj(7, HEAD_W)


def _inproj(x, norm_w, w_all, cos_t, sin_t, conv_w, tm):
    b, t, d = x.shape
    grid = (b, t // tm)
    tok = lambda width, dt: jax.ShapeDtypeStruct((b, t, width), dt)
    tok_spec = lambda width: pl.BlockSpec((1, tm, width), lambda i, j: (i, j, 0))
    const = lambda shape: pl.BlockSpec(shape, lambda i, j: (0,) * len(shape),
                                       pipeline_mode=pl.Buffered(1))
    return pl.pallas_call(
        functools.partial(_inproj_kernel, tm=tm),
        grid=grid,
        in_specs=[tok_spec(d), const((1, d)), const(w_all.shape),
                  pl.BlockSpec((tm, HEAD_W), lambda i, j: (j, 0)),
                  pl.BlockSpec((tm, HEAD_W), lambda i, j: (j, 0)),
                  const(conv_w.shape)],
        out_specs=[tok_spec(GROUP_W)] * 7 + [tok_spec(HEAD_W)],
        out_shape=[tok(GROUP_W, BF16)] * 7 + [tok(HEAD_W, F32)],
        scratch_shapes=[pltpu.VMEM((3, 8 + tm, GROUP_W), F32),
                        pltpu.VMEM((8, 3 * GROUP_W), F32)],
        compiler_params=pltpu.CompilerParams(
            dimension_semantics=("parallel", "arbitrary"), vmem_limit_bytes=VMEM_LIMIT),
        name="inproj",
    )(x, norm_w, w_all, cos_t, sin_t, conv_w)


def _attn_kernel(q_ref, k_ref, v_ref, lam_ref, sw_ref, o_ref, m_ref, l_ref, a_ref, *, tq, tk):
    i = pl.program_id(1)
    lane = lax.broadcasted_iota(jnp.int32, (1, HEAD_W), 1)
    first_map = (lane % 64) < 32
    head_cols = [slice(h * HEAD_W, (h + 1) * HEAD_W) for h in range(DA_HEADS)]
    streams = []
    for h in range(DA_HEADS):
        q = q_ref[0, :, head_cols[h]]
        zero = jnp.zeros_like(q)
        streams.append((h, 2 * h, jnp.where(first_map, q, zero)))
        streams.append((h, 2 * h + 1, jnp.where(first_map, zero, q)))
    m_ref[...] = jnp.full_like(m_ref, NEG_BIG)
    l_ref[...] = jnp.zeros_like(l_ref)
    a_ref[...] = jnp.zeros_like(a_ref)

    def block(st, width, masked):
        kb = [k_ref[0, pl.ds(st, width), head_cols[h]] for h in range(DA_HEADS)]
        vb = [v_ref[0, pl.ds(st, width), head_cols[h]] for h in range(DA_HEADS)]
        scores = [_dot_nt(qm, kb[h]) for h, _, qm in streams]
        for s, (h, si, _) in zip(scores, streams):
            if masked:
                row = lax.broadcasted_iota(jnp.int32, (tq, width), 0)
                col = lax.broadcasted_iota(jnp.int32, (tq, width), 1)
                s = jnp.where(col <= row + (width - tq), s, NEG_BIG)
            m_old = m_ref[si]
            m_new = jnp.maximum(m_old, jnp.max(s, axis=-1, keepdims=True))
            alpha = jnp.exp2(m_old - m_new)
            p = jnp.concatenate([jnp.exp2(s[:, c * HEAD_W:(c + 1) * HEAD_W] - m_new)
                                 for c in range(width // HEAD_W)], axis=-1)
            l_new = alpha * l_ref[si]
            for c in range(width // HEAD_W):
                l_new = l_new + p[:, c * HEAD_W:(c + 1) * HEAD_W]
            m_ref[si] = m_new
            l_ref[si] = l_new
            a_ref[si] = alpha * a_ref[si] + _dot(p.astype(BF16), vb[h])

    def body(j, carry):
        block(pl.multiple_of(j * tk, tk), tk, False)
        return carry

    lax.fori_loop(0, (i * tq) // tk, body, 0)

    @pl.when(i % 2 == 1)
    def _():
        block(pl.multiple_of((i - 1) * tq, tk), tk, True)

    @pl.when(i % 2 == 0)
    def _():
        block(pl.multiple_of(i * tq, tk), tq, True)

    lp = lam_ref[...]
    lam = (jnp.exp(jnp.sum(lp[0:1] * lp[1:2], axis=-1, keepdims=True))
           - jnp.exp(jnp.sum(lp[2:3] * lp[3:4], axis=-1, keepdims=True)) + LAMBDA_INIT)
    for h in range(DA_HEADS):
        r1 = 1.0 / jnp.sum(l_ref[2 * h], axis=-1, keepdims=True)
        r2 = 1.0 / jnp.sum(l_ref[2 * h + 1], axis=-1, keepdims=True)
        o = a_ref[2 * h] * r1 - lam * (a_ref[2 * h + 1] * r2)
        o_ref[0, :, head_cols[h]] = (_rms(o, sw_ref[...]) * (1.0 - LAMBDA_INIT)).astype(BF16)


def _attention(q, k, v, lam_p, subln_w, tq):
    b, t, _ = q.shape
    tk = 2 * tq
    assert t % tq == 0
    n_streams = 2 * DA_HEADS
    return pl.pallas_call(
        functools.partial(_attn_kernel, tq=tq, tk=tk),
        grid=(b, t // tq),
        in_specs=[pl.BlockSpec((1, tq, GROUP_W), lambda bi, i: (bi, i, 0)),
                  pl.BlockSpec((1, t, GROUP_W), lambda bi, i: (bi, 0, 0)),
                  pl.BlockSpec((1, t, GROUP_W), lambda bi, i: (bi, 0, 0)),
                  pl.BlockSpec(lam_p.shape, lambda bi, i: (0, 0)),
                  pl.BlockSpec((1, HEAD_W), lambda bi, i: (0, 0))],
        out_specs=pl.BlockSpec((1, tq, GROUP_W), lambda bi, i: (bi, i, 0)),
        out_shape=jax.ShapeDtypeStruct((b, t, GROUP_W), BF16),
        scratch_shapes=[pltpu.VMEM((n_streams, tq, HEAD_W), F32)] * 3,
        compiler_params=pltpu.CompilerParams(
            dimension_semantics=("parallel", "arbitrary"), vmem_limit_bytes=VMEM_LIMIT),
        name="diff_attn",
    )(q, k, v, lam_p, subln_w)


def _gdn_kernel(gq_ref, gk_ref, gv_ref, gz_ref, gba_ref, alog_ref, dtb_ref, nw_ref, o_ref,
                s_ref, u_ref, w_ref, qk_ref, qd_ref, kdt_ref, gl_ref, *, tt, c):
    nh = GDN_HEADS
    nc = tt // c

    @pl.when(pl.program_id(1) == 0)
    def _():
        s_ref[...] = jnp.zeros_like(s_ref)

    gba = gba_ref[0]
    beta_all = _sigmoid(gba)
    xa = gba + dtb_ref[...]
    g_all = -jnp.exp(alog_ref[...]) * (jnp.maximum(xa, 0.0) + jnp.log1p(jnp.exp(-jnp.abs(xa))))

    r_i = lax.broadcasted_iota(jnp.int32, (c, c), 0)
    c_i = lax.broadcasted_iota(jnp.int32, (c, c), 1)
    tril = c_i <= r_i
    strict = c_i < r_i
    eye = (c_i == r_i).astype(F32)
    tril_f = tril.astype(F32)
    blk_mask = [(r_i // INV_BLOCK) == (c_i // INV_BLOCK)]
    size = INV_BLOCK
    while size < c:
        blk_mask.append(((r_i // (2 * size)) == (c_i // (2 * size))) & ((r_i // size) > (c_i // size)))
        size *= 2

    def level(fn, items):
        return [fn(x) for x in items]

    for g0 in range(0, nc, GDN_GROUP):
        prep = []
        for ci in range(g0, g0 + GDN_GROUP):
            rows = slice(ci * c, (ci + 1) * c)
            gcum = jnp.dot(tril_f, g_all[rows], preferred_element_type=F32, precision=lax.Precision.HIGHEST)
            gcum_t = gcum.T
            for h in range(nh):
                cols = slice(h * HEAD_W, (h + 1) * HEAD_W)
                q_bf = gq_ref[0, rows, cols]
                k_bf = gk_ref[0, rows, cols]
                q = q_bf.astype(F32)
                k = k_bf.astype(F32)
                v = gv_ref[0, rows, cols].astype(F32)
                beta = beta_all[rows, h:h + 1]
                gc = gcum[:, nh + h:nh + h + 1]
                gr = gcum_t[nh + h:nh + h + 1, :]
                g_last = gcum[c - 1:c, nh + h:nh + h + 1]
                decay = jnp.exp(jnp.where(tril, gc - gr, -jnp.inf))
                e_gc = jnp.exp(gc)
                kb = k * beta
                idx = ci * nh + h
                qd_ref[idx] = (q * e_gc).astype(BF16)
                kdt_ref[idx] = (k * jnp.exp(g_last - gc)).T.astype(BF16)
                gl_ref[idx] = jnp.broadcast_to(jnp.exp(g_last), (8, HEAD_W))
                rhs = jnp.concatenate([v * beta, kb * e_gc], axis=-1).astype(BF16)
                prep.append((idx, q_bf, k_bf, kb.astype(BF16), decay, rhs))
        kkt = level(lambda a: _dot_nt(a[3], a[2]) * a[4], prep)
        for a, qk in zip(prep, level(lambda a: _dot_nt(a[1], a[2]) * a[4], prep)):
            qk_ref[a[0]] = qk.astype(BF16)
        lmat = [jnp.where(strict, x, 0.0) for x in kkt]
        p = [jnp.where(blk_mask[0], -x, 0.0) for x in lmat]
        t_inv = [eye + x for x in p]
        for _ in range(int(math.log2(INV_BLOCK)) - 1):
            p = level(lambda x: _dot(x.astype(BF16), x.astype(BF16)), p)
            t_inv = level(lambda tx: tx[0] + _dot(tx[0].astype(BF16), tx[1].astype(BF16)), list(zip(t_inv, p)))
        for off_mask in blk_mask[1:]:
            t_bf = [x.astype(BF16) for x in t_inv]
            x_off = level(lambda tl: _dot(tl[0], jnp.where(off_mask, tl[1], 0.0).astype(BF16)).astype(BF16),
                          list(zip(t_bf, lmat)))
            t_inv = level(lambda txb: txb[0] - _dot(txb[1], txb[2]), list(zip(t_inv, x_off, t_bf)))
        for a, sol in zip(prep, level(lambda ta: _dot(ta[0].astype(BF16), ta[1][5]), list(zip(t_inv, prep)))):
            u_ref[a[0]] = sol[:, :HEAD_W]
            w_ref[a[0]] = sol[:, HEAD_W:].astype(BF16)

    nw = nw_ref[...]
    heads = range(nh)
    for ci in range(nc):
        rows = slice(ci * c, (ci + 1) * c)
        s_old = [s_ref[h] for h in heads]
        s_bf = [x.astype(BF16) for x in s_old]
        ws = [_dot(w_ref[ci * nh + h], s_bf[h]) for h in heads]
        qs = [_dot(qd_ref[ci * nh + h], s_bf[h]) for h in heads]
        v_bf = [(u_ref[ci * nh + h] - ws[h]).astype(BF16) for h in heads]
        kv = [_dot(kdt_ref[ci * nh + h], v_bf[h]) for h in heads]
        for h in heads:
            s_ref[h] = s_old[h] * gl_ref[ci * nh + h][0:1, :] + kv[h]
        o = [qs[h] + _dot(qk_ref[ci * nh + h], v_bf[h]) for h in heads]
        for h in heads:
            cols = slice(h * HEAD_W, (h + 1) * HEAD_W)
            z = gz_ref[0, rows, cols].astype(F32)
            o_ref[0, rows, cols] = (_rms(o[h], nw) * _silu(z)).astype(BF16)


def _gdn(gq, gk, gv, gz, gba, alog_row, dtb_row, norm_w, tt, c):
    b, t, _ = gq.shape
    nidx = (tt // c) * GDN_HEADS
    tok_spec = lambda width: pl.BlockSpec((1, tt, width), lambda i, j: (i, j, 0))
    const = lambda shape: pl.BlockSpec(shape, lambda i, j: (0,) * len(shape))
    return pl.pallas_call(
        functools.partial(_gdn_kernel, tt=tt, c=c),
        grid=(b, t // tt),
        in_specs=[tok_spec(GROUP_W)] * 4 + [tok_spec(HEAD_W)] + [const((1, HEAD_W))] * 3,
        out_specs=tok_spec(GROUP_W),
        out_shape=jax.ShapeDtypeStruct((b, t, GROUP_W), BF16),
        scratch_shapes=[pltpu.VMEM((GDN_HEADS, HEAD_W, HEAD_W), F32),
                        pltpu.VMEM((nidx, c, HEAD_W), F32),
                        pltpu.VMEM((nidx, c, HEAD_W), BF16),
                        pltpu.VMEM((nidx, c, c), BF16),
                        pltpu.VMEM((nidx, c, HEAD_W), BF16),
                        pltpu.VMEM((nidx, HEAD_W, c), BF16),
                        pltpu.VMEM((nidx, 8, HEAD_W), F32)],
        compiler_params=pltpu.CompilerParams(
            dimension_semantics=("parallel", "arbitrary"), vmem_limit_bytes=VMEM_LIMIT),
        name="gdn",
    )(gq, gk, gv, gz, gba, alog_row, dtb_row, norm_w)


def _ffn_kernel(x_ref, yda_ref, ygdn_ref, wout_ref, fnw_ref, wup_ref, cw_ref, wdown_ref, finw_ref, o_ref,
                hx_refs, carry_ref, *, tm):
    n_chunks = D_FF // FF_CHUNK
    cw2 = 2 * FF_CHUNK

    @pl.when(pl.program_id(1) == 0)
    def _():
        carry_ref[...] = jnp.zeros_like(carry_ref)

    h = (x_ref[0] + _dot(yda_ref[0], wout_ref[0:GROUP_W, :])
         + _dot(ygdn_ref[0], wout_ref[GROUP_W:2 * GROUP_W, :]))
    hn = _rms(h, fnw_ref[...]).astype(BF16)
    acc = jnp.zeros((tm, D_MODEL), F32)

    halves = (slice(0, FF_CHUNK), slice(FF_CHUNK, cw2))

    def chunk_cols(ci):
        return (slice(ci * FF_CHUNK, (ci + 1) * FF_CHUNK), slice(D_FF + ci * FF_CHUNK, D_FF + (ci + 1) * FF_CHUNK))

    def up_proj(ci):
        hx_ref = hx_refs.at[ci % FF_WINDOWS]
        for half, cols in zip(halves, chunk_cols(ci)):
            hx_ref[0:8, half] = carry_ref[:, cols]
            hx_ref[8:8 + tm, half] = _dot(hn, wup_ref[:, cols])
            carry_ref[:, cols] = hx_ref[tm:tm + 8, half]

    for ci in range(FF_WINDOWS - 1):
        up_proj(ci)
    for ci in range(n_chunks):
        if ci + FF_WINDOWS - 1 < n_chunks:
            up_proj(ci + FF_WINDOWS - 1)
        hx_ref = hx_refs.at[ci % FF_WINDOWS]
        cw = jnp.concatenate([cw_ref[:, cols] for cols in chunk_cols(ci)], axis=-1)
        y = cw[FFN_CONV - 1:FFN_CONV] * hx_ref[8:8 + tm, :]
        for j in range(1, FFN_CONV):
            y = y + cw[FFN_CONV - 1 - j:FFN_CONV - j] * hx_ref[8 - j:8 - j + tm, :]
        hg = 0.5 * y[:, :FF_CHUNK]
        act = (hg * (1.0 + jnp.tanh(hg)) * y[:, FF_CHUNK:]).astype(BF16)
        acc = acc + _dot(act, wdown_ref[ci * FF_CHUNK:(ci + 1) * FF_CHUNK, :])
    o_ref[0] = _rms(h + acc, finw_ref[...])


def _ffn(x, y_da, y_gdn, w_out, ffn_norm_w, w_up, conv_w, w_down, final_norm_w, tm):
    b, t, d = x.shape
    tok_spec = lambda width: pl.BlockSpec((1, tm, width), lambda i, j: (i, j, 0))
    const = lambda shape: pl.BlockSpec(shape, lambda i, j: (0,) * len(shape),
                                       pipeline_mode=pl.Buffered(1))
    return pl.pallas_call(
        functools.partial(_ffn_kernel, tm=tm),
        grid=(b, t // tm),
        in_specs=[tok_spec(d), tok_spec(GROUP_W), tok_spec(GROUP_W), const(w_out.shape), const((1, d)),
                  const(w_up.shape), const(conv_w.shape), const(w_down.shape), const((1, d))],
        out_specs=tok_spec(d),
        out_shape=jax.ShapeDtypeStruct((b, t, d), F32),
        scratch_shapes=[pltpu.VMEM((FF_WINDOWS, 8 + tm, 2 * FF_CHUNK), F32),
                        pltpu.VMEM((8, 2 * D_FF), F32)],
        compiler_params=pltpu.CompilerParams(
            dimension_semantics=("parallel", "arbitrary"), vmem_limit_bytes=VMEM_LIMIT),
        name="outproj_ffn",
    )(x, y_da, y_gdn, w_out, ffn_norm_w, w_up, conv_w, w_down, final_norm_w)


def _permute_qk_cols(w):
    half = DA_HEAD_DIM // 2
    rows = w.shape[0]
    return w.reshape(rows, DA_HEADS, 2, 2, half).transpose(0, 1, 3, 2, 4).reshape(rows, GROUP_W)


def _rope_tables(t):
    half = DA_HEAD_DIM // 2
    inv_freq = ROPE_THETA ** (-jnp.arange(0, DA_HEAD_DIM, 2, dtype=F32) / DA_HEAD_DIM)
    ang = jnp.arange(t, dtype=F32)[:, None] * inv_freq[None, :]
    cos, sin = jnp.cos(ang), jnp.sin(ang)
    assert cos.shape == (t, half)
    return jnp.tile(cos, (1, 4)), jnp.concatenate([-sin, -sin, sin, sin], axis=-1)


def _tile(t, pref):
    while t % pref:
        pref //= 2
    return pref


def kernel(x, attn_norm_w, w_in, da_lambda_q1, da_lambda_k1, da_lambda_q2, da_lambda_k2, da_subln_w,
           gdn_conv_w, gdn_a_log, gdn_dt_bias, gdn_norm_w, w_out, ffn_norm_w, ffn_w_up, ffn_conv_w,
           ffn_w_down, final_norm_w):
    b, t, d = x.shape
    assert d == D_MODEL and t % GDN_CHUNK == 0
    l = 0
    w = w_in[l]
    nh = GDN_HEADS
    small = jnp.pad(w[:, 7 * GROUP_W:], ((0, 0), (0, HEAD_W - 2 * nh)))
    w_all = jnp.concatenate([_permute_qk_cols(w[:, :GROUP_W]), _permute_qk_cols(w[:, GROUP_W:2 * GROUP_W]),
                             w[:, 2 * GROUP_W:7 * GROUP_W], small], axis=-1).astype(BF16)
    cos_t, sin_t = _rope_tables(t)

    q, k, v, gq, gk, gv, gz, gba = _inproj(x, attn_norm_w[l][None, :], w_all, cos_t, sin_t, gdn_conv_w[l],
                                           _tile(t, 512))

    lam_p = jnp.stack([da_lambda_q1[l], da_lambda_k1[l], da_lambda_q2[l], da_lambda_k2[l]])
    y_da = _attention(q, k, v, lam_p, da_subln_w[l][None, :], _tile(t, 512))

    lane_pad = lambda vec: jnp.pad(vec, (nh, HEAD_W - 2 * nh))[None, :]
    y_gdn = _gdn(gq, gk, gv, gz, gba, lane_pad(gdn_a_log[l]), lane_pad(gdn_dt_bias[l]),
                 gdn_norm_w[l][None, :], _tile(t, 512), GDN_CHUNK)

    return _ffn(x, y_da, y_gdn, w_out[l].astype(BF16), ffn_norm_w[l][None, :],
                ffn_w_up[l].astype(BF16), ffn_conv_w[l],
                ffn_w_down[l].astype(BF16), final_norm_w[None, :], _tile(t, 512))
```

```python
import functools
import math

import jax
import jax.numpy as jnp
from jax import lax
from jax.experimental import pallas as pl
from jax.experimental.pallas import tpu as pltpu

F32 = jnp.float32
BF16 = jnp.bfloat16

D_MODEL = 1024
DA_HEADS = 4
DA_HEAD_DIM = 64
HEAD_W = 128
GROUP_W = 512
GDN_HEADS = 4
SHORT_CONV = 4
D_FF = 2816
FFN_CONV = 3
FF_CHUNK = 256
FF_WINDOWS = 4
ROPE_THETA = 10000.0
EPS = 1e-6
LAMBDA_INIT = 0.8 - 0.6 * math.exp(-0.3 * 0)
GDN_CHUNK = 128
INV_BLOCK = 16
GDN_GROUP = 4
NEG_BIG = -1e30
LOG2E = math.log2(math.e)
VMEM_LIMIT = 56 * 1024 * 1024


def _rms(x, w):
    return x * lax.rsqrt(jnp.mean(x * x, axis=-1, keepdims=True) + EPS) * w


def _dot(a, b):
    return jnp.dot(a, b, preferred_element_type=F32)


def _dot_nt(a, b):
    return lax.dot_general(a, b, (((1,), (1,)), ((), ())), preferred_element_type=F32)


def _sigmoid(x):
    return 1.0 / (1.0 + jnp.exp(-x))


def _silu(x):
    hx = 0.5 * x
    return hx * (1.0 + jnp.tanh(hx))


def _inproj_kernel(x_ref, nw_ref, w_ref, cos_ref, sin_ref, cw_ref,
                   q_ref, k_ref, v_ref, gq_ref, gk_ref, gv_ref, gz_ref, gba_ref,
                   win_refs, carry_ref, *, tm):
    @pl.when(pl.program_id(1) == 0)
    def _():
        carry_ref[...] = jnp.zeros_like(carry_ref)

    xn = _rms(x_ref[0], nw_ref[...]).astype(BF16)
    cos = cos_ref[...]
    sin = sin_ref[...]

    def proj(seg, width=GROUP_W):
        return _dot(xn, w_ref[:, seg * GROUP_W: seg * GROUP_W + width])

    def rope(p, scale):
        outs = []
        for h in range(DA_HEADS):
            ph = p[:, h * HEAD_W:(h + 1) * HEAD_W]
            outs.append((ph * cos + pltpu.roll(ph, HEAD_W // 2, 1) * sin) * scale)
        return jnp.concatenate(outs, axis=-1)

    q_ref[0] = rope(proj(0), DA_HEAD_DIM ** -0.5 * LOG2E).astype(BF16)
    k_ref[0] = rope(proj(1), 1.0).astype(BF16)
    v_ref[0] = proj(2).astype(BF16)

    def short_conv_silu(seg, slot, out_ref, l2_scale):
        cols = slice(slot * GROUP_W, (slot + 1) * GROUP_W)
        win = win_refs.at[slot]
        win[0:8, :] = carry_ref[:, cols]
        win[8:8 + tm, :] = proj(seg)
        carry_ref[:, cols] = win[tm:tm + 8, :]
        for h in range(GDN_HEADS):
            hc = slice(h * HEAD_W, (h + 1) * HEAD_W)
            cw = cw_ref[:, slot * GROUP_W + h * HEAD_W: slot * GROUP_W + (h + 1) * HEAD_W]
            y = cw[SHORT_CONV - 1:SHORT_CONV] * win[8:8 + tm, hc]
            for j in range(1, SHORT_CONV):
                y = y + cw[SHORT_CONV - 1 - j:SHORT_CONV - j] * win[8 - j:8 - j + tm, hc]
            y = _silu(y)
            if l2_scale is not None:
                y = y * (lax.rsqrt(jnp.sum(y * y, axis=-1, keepdims=True) + EPS) * l2_scale)
            out_ref[0, :, hc] = y.astype(BF16)

    short_conv_silu(3, 0, gq_ref, HEAD_W ** -0.5)
    short_conv_silu(4, 1, gk_ref, 1.0)
    short_conv_silu(5, 2, gv_ref, None)
    gz_ref[0] = proj(6).astype(BF16)
    gba_ref[0] = proj(7, HEAD_W)


def _inproj(x, norm_w, w_all, cos_t, sin_t, conv_w, tm):
    b, t, d = x.shape
    grid = (b, t // tm)
    tok = lambda width, dt: jax.ShapeDtypeStruct((b, t, width), dt)
    tok_spec = lambda width: pl.BlockSpec((1, tm, width), lambda i, j: (i, j, 0))
    const = lambda shape: pl.BlockSpec(shape, lambda i, j: (0,) * len(shape),
                                       pipeline_mode=pl.Buffered(1))
    return pl.pallas_call(
        functools.partial(_inproj_kernel, tm=tm),
        grid=grid,
        in_specs=[tok_spec(d), const((1, d)), const(w_all.shape),
                  pl.BlockSpec((tm, HEAD_W), lambda i, j: (j, 0)),
                  pl.BlockSpec((tm, HEAD_W), lambda i, j: (j, 0)),
                  const(conv_w.shape)],
        out_specs=[tok_spec(GROUP_W)] * 7 + [tok_spec(HEAD_W)],
        out_shape=[tok(GROUP_W, BF16)] * 7 + [tok(HEAD_W, F32)],
        scratch_shapes=[pltpu.VMEM((3, 8 + tm, GROUP_W), F32),
                        pltpu.VMEM((8, 3 * GROUP_W), F32)],
        compiler_params=pltpu.CompilerParams(
            dimension_semantics=("parallel", "arbitrary"), vmem_limit_bytes=VMEM_LIMIT),
        name="inproj",
    )(x, norm_w, w_all, cos_t, sin_t, conv_w)


def _attn_kernel(q_ref, k_ref, v_ref, lam_ref, sw_ref, o_ref, m_ref, l_ref, a_ref, *, tq, tk):
    i = pl.program_id(1)
    lane = lax.broadcasted_iota(jnp.int32, (1, HEAD_W), 1)
    first_map = (lane % 64) < 32
    head_cols = [slice(h * HEAD_W, (h + 1) * HEAD_W) for h in range(DA_HEADS)]
    streams = []
    for h in range(DA_HEADS):
        q = q_ref[0, :, head_cols[h]]
        zero = jnp.zeros_like(q)
        streams.append((h, 2 * h, jnp.where(first_map, q, zero)))
        streams.append((h, 2 * h + 1, jnp.where(first_map, zero, q)))
    def block(st, width, diag):
        for r0, nr in (((0, tq // 2), (tq // 2, tq // 2)) if diag else ((0, tq),)):
            rows = slice(r0, r0 + nr)
            ncols = width - tq + r0 + nr if diag else width
            kb = [k_ref[0, pl.ds(st, ncols), head_cols[h]] for h in range(DA_HEADS)]
            vb = [v_ref[0, pl.ds(st, ncols), head_cols[h]] for h in range(DA_HEADS)]
            scores = [_dot_nt(qm[rows], kb[h]) for h, _, qm in streams]
            for s, (h, si, _) in zip(scores, streams):
                if diag:
                    row = lax.broadcasted_iota(jnp.int32, (nr, ncols), 0) + (r0 + width - tq)
                    col = lax.broadcasted_iota(jnp.int32, (nr, ncols), 1)
                    s = jnp.where(col <= row, s, NEG_BIG)
                    m_new = jnp.broadcast_to(jnp.max(s, axis=-1, keepdims=True), (nr, HEAD_W))
                else:
                    m_old = m_ref[si, rows]
                    m_new = jnp.maximum(m_old, jnp.max(s, axis=-1, keepdims=True))
                p = jnp.concatenate([jnp.exp2(s[:, c * HEAD_W:(c + 1) * HEAD_W] - m_new)
                                     for c in range(ncols // HEAD_W)], axis=-1)
                l_new = p[:, :HEAD_W]
                for c in range(1, ncols // HEAD_W):
                    l_new = l_new + p[:, c * HEAD_W:(c + 1) * HEAD_W]
                pv = _dot(p.astype(BF16), vb[h])
                if diag:
                    l_ref[si, rows] = l_new
                    a_ref[si, rows] = pv
                else:
                    alpha = jnp.exp2(m_old - m_new)
                    l_ref[si, rows] = alpha * l_ref[si, rows] + l_new
                    a_ref[si, rows] = alpha * a_ref[si, rows] + pv
                m_ref[si, rows] = m_new

    @pl.when(i % 2 == 1)
    def _():
        block(pl.multiple_of((i - 1) * tq, tk), tk, True)

    @pl.when(i % 2 == 0)
    def _():
        block(pl.multiple_of(i * tq, tk), tq, True)

    def body(j, carry):
        block(pl.multiple_of(j * tk, tk), tk, False)
        return carry

    lax.fori_loop(0, (i * tq) // tk, body, 0)

    lp = lam_ref[...]
    lam = (jnp.exp(jnp.sum(lp[0:1] * lp[1:2], axis=-1, keepdims=True))
           - jnp.exp(jnp.sum(lp[2:3] * lp[3:4], axis=-1, keepdims=True)) + LAMBDA_INIT)
    for h in range(DA_HEADS):
        r1 = 1.0 / jnp.sum(l_ref[2 * h], axis=-1, keepdims=True)
        r2 = 1.0 / jnp.sum(l_ref[2 * h + 1], axis=-1, keepdims=True)
        o = a_ref[2 * h] * r1 - lam * (a_ref[2 * h + 1] * r2)
        o_ref[0, :, head_cols[h]] = (_rms(o, sw_ref[...]) * (1.0 - LAMBDA_INIT)).astype(BF16)


def _attention(q, k, v, lam_p, subln_w, tq):
    b, t, _ = q.shape
    tk = 2 * tq
    assert t % tq == 0 and tq % (2 * HEAD_W) == 0
    n_streams = 2 * DA_HEADS
    return pl.pallas_call(
        functools.partial(_attn_kernel, tq=tq, tk=tk),
        grid=(b, t // tq),
        in_specs=[pl.BlockSpec((1, tq, GROUP_W), lambda bi, i: (bi, i, 0)),
                  pl.BlockSpec((1, t, GROUP_W), lambda bi, i: (bi, 0, 0)),
                  pl.BlockSpec((1, t, GROUP_W), lambda bi, i: (bi, 0, 0)),
                  pl.BlockSpec(lam_p.shape, lambda bi, i: (0, 0)),
                  pl.BlockSpec((1, HEAD_W), lambda bi, i: (0, 0))],
        out_specs=pl.BlockSpec((1, tq, GROUP_W), lambda bi, i: (bi, i, 0)),
        out_shape=jax.ShapeDtypeStruct((b, t, GROUP_W), BF16),
        scratch_shapes=[pltpu.VMEM((n_streams, tq, HEAD_W), F32)] * 3,
        compiler_params=pltpu.CompilerParams(
            dimension_semantics=("parallel", "arbitrary"), vmem_limit_bytes=VMEM_LIMIT),
        name="diff_attn",
    )(q, k, v, lam_p, subln_w)


def _gdn_kernel(gq_ref, gk_ref, gv_ref, gz_ref, gba_ref, alog_ref, dtb_ref, nw_ref, o_ref,
                s_ref, u_ref, w_ref, qk_ref, qd_ref, kdt_ref, gl_ref, *, tt, c):
    nh = GDN_HEADS
    nc = tt // c

    @pl.when(pl.program_id(1) == 0)
    def _():
        s_ref[...] = jnp.zeros_like(s_ref)

    gba = gba_ref[0]
    beta_all = _sigmoid(gba)
    xa = gba + dtb_ref[...]
    g_all = -jnp.exp(alog_ref[...]) * (jnp.maximum(xa, 0.0) + jnp.log1p(jnp.exp(-jnp.abs(xa))))

    r_i = lax.broadcasted_iota(jnp.int32, (c, c), 0)
    c_i = lax.broadcasted_iota(jnp.int32, (c, c), 1)
    tril = c_i <= r_i
    strict = c_i < r_i
    eye = (c_i == r_i).astype(F32)
    tril_f = tril.astype(F32)
    blk_mask = [(r_i // INV_BLOCK) == (c_i // INV_BLOCK)]
    size = INV_BLOCK
    while size < c:
        blk_mask.append(((r_i // (2 * size)) == (c_i // (2 * size))) & ((r_i // size) > (c_i // size)))
        size *= 2

    def level(fn, items):
        return [fn(x) for x in items]

    for g0 in range(0, nc, GDN_GROUP):
        prep = []
        for ci in range(g0, g0 + GDN_GROUP):
            rows = slice(ci * c, (ci + 1) * c)
            gcum = jnp.dot(tril_f, g_all[rows], preferred_element_type=F32, precision=lax.Precision.HIGHEST)
            gcum_t = gcum.T
            for h in range(nh):
                cols = slice(h * HEAD_W, (h + 1) * HEAD_W)
                q_bf = gq_ref[0, rows, cols]
                k_bf = gk_ref[0, rows, cols]
                q = q_bf.astype(F32)
                k = k_bf.astype(F32)
                v = gv_ref[0, rows, cols].astype(F32)
                beta = beta_all[rows, h:h + 1]
                gc = gcum[:, nh + h:nh + h + 1]
                gr = gcum_t[nh + h:nh + h + 1, :]
                g_last = gcum[c - 1:c, nh + h:nh + h + 1]
                decay = jnp.exp(jnp.where(tril, gc - gr, -jnp.inf))
                e_gc = jnp.exp(gc)
                kb = k * beta
                idx = ci * nh + h
                qd_ref[idx] = (q * e_gc).astype(BF16)
                kdt_ref[idx] = (k * jnp.exp(g_last - gc)).T.astype(BF16)
                gl_ref[idx] = jnp.broadcast_to(jnp.exp(g_last), (8, HEAD_W))
                rhs = jnp.concatenate([v * beta, kb * e_gc], axis=-1).astype(BF16)
                prep.append((idx, q_bf, k_bf, kb.astype(BF16), decay, rhs))
        kkt = level(lambda a: _dot_nt(a[3], a[2]) * a[4], prep)
        for a, qk in zip(prep, level(lambda a: _dot_nt(a[1], a[2]) * a[4], prep)):
            qk_ref[a[0]] = qk.astype(BF16)
        lmat = [jnp.where(strict, x, 0.0) for x in kkt]
        p = [jnp.where(blk_mask[0], -x, 0.0) for x in lmat]
        t_inv = [eye + x for x in p]
        for _ in range(int(math.log2(INV_BLOCK)) - 1):
            p = level(lambda x: _dot(x.astype(BF16), x.astype(BF16)), p)
            t_inv = level(lambda tx: tx[0] + _dot(tx[0].astype(BF16), tx[1].astype(BF16)), list(zip(t_inv, p)))
        for off_mask in blk_mask[1:]:
            t_bf = [x.astype(BF16) for x in t_inv]
            x_off = level(lambda tl: _dot(tl[0], jnp.where(off_mask, tl[1], 0.0).astype(BF16)).astype(BF16),
                          list(zip(t_bf, lmat)))
            t_inv = level(lambda txb: txb[0] - _dot(txb[1], txb[2]), list(zip(t_inv, x_off, t_bf)))
        for a, sol in zip(prep, level(lambda ta: _dot(ta[0].astype(BF16), ta[1][5]), list(zip(t_inv, prep)))):
            u_ref[a[0]] = sol[:, :HEAD_W]
            w_ref[a[0]] = sol[:, HEAD_W:].astype(BF16)

    nw = nw_ref[...]
    heads = range(nh)
    for ci in range(nc):
        rows = slice(ci * c, (ci + 1) * c)
        s_old = [s_ref[h] for h in heads]
        s_bf = [x.astype(BF16) for x in s_old]
        ws = [_dot(w_ref[ci * nh + h], s_bf[h]) for h in heads]
        qs = [_dot(qd_ref[ci * nh + h], s_bf[h]) for h in heads]
        v_bf = [(u_ref[ci * nh + h] - ws[h]).astype(BF16) for h in heads]
        kv = [_dot(kdt_ref[ci * nh + h], v_bf[h]) for h in heads]
        for h in heads:
            s_ref[h] = s_old[h] * gl_ref[ci * nh + h][0:1, :] + kv[h]
        o = [qs[h] + _dot(qk_ref[ci * nh + h], v_bf[h]) for h in heads]
        for h in heads:
            cols = slice(h * HEAD_W, (h + 1) * HEAD_W)
            z = gz_ref[0, rows, cols].astype(F32)
            o_ref[0, rows, cols] = (_rms(o[h], nw) * _silu(z)).astype(BF16)


def _gdn(gq, gk, gv, gz, gba, alog_row, dtb_row, norm_w, tt, c):
    b, t, _ = gq.shape
    nidx = (tt // c) * GDN_HEADS
    tok_spec = lambda width: pl.BlockSpec((1, tt, width), lambda i, j: (i, j, 0))
    const = lambda shape: pl.BlockSpec(shape, lambda i, j: (0,) * len(shape))
    return pl.pallas_call(
        functools.partial(_gdn_kernel, tt=tt, c=c),
        grid=(b, t // tt),
        in_specs=[tok_spec(GROUP_W)] * 4 + [tok_spec(HEAD_W)] + [const((1, HEAD_W))] * 3,
        out_specs=tok_spec(GROUP_W),
        out_shape=jax.ShapeDtypeStruct((b, t, GROUP_W), BF16),
        scratch_shapes=[pltpu.VMEM((GDN_HEADS, HEAD_W, HEAD_W), F32),
                        pltpu.VMEM((nidx, c, HEAD_W), F32),
                        pltpu.VMEM((nidx, c, HEAD_W), BF16),
                        pltpu.VMEM((nidx, c, c), BF16),
                        pltpu.VMEM((nidx, c, HEAD_W), BF16),
                        pltpu.VMEM((nidx, HEAD_W, c), BF16),
                        pltpu.VMEM((nidx, 8, HEAD_W), F32)],
        compiler_params=pltpu.CompilerParams(
            dimension_semantics=("parallel", "arbitrary"), vmem_limit_bytes=VMEM_LIMIT),
        name="gdn",
    )(gq, gk, gv, gz, gba, alog_row, dtb_row, norm_w)


def _ffn_kernel(x_ref, yda_ref, ygdn_ref, wout_ref, fnw_ref, wup_ref, cw_ref, wdown_ref, finw_ref, o_ref,
                hx_refs, carry_ref, *, tm):
    n_chunks = D_FF // FF_CHUNK
    cw2 = 2 * FF_CHUNK

    @pl.when(pl.program_id(1) == 0)
    def _():
        carry_ref[...] = jnp.zeros_like(carry_ref)

    h = (x_ref[0] + _dot(yda_ref[0], wout_ref[0:GROUP_W, :])
         + _dot(ygdn_ref[0], wout_ref[GROUP_W:2 * GROUP_W, :]))
    hn = _rms(h, fnw_ref[...]).astype(BF16)
    acc = jnp.zeros((tm, D_MODEL), F32)

    halves = (slice(0, FF_CHUNK), slice(FF_CHUNK, cw2))

    def chunk_cols(ci):
        return (slice(ci * FF_CHUNK, (ci + 1) * FF_CHUNK), slice(D_FF + ci * FF_CHUNK, D_FF + (ci + 1) * FF_CHUNK))

    def up_proj(ci):
        hx_ref = hx_refs.at[ci % FF_WINDOWS]
        for half, cols in zip(halves, chunk_cols(ci)):
            hx_ref[0:8, half] = carry_ref[:, cols]
            hx_ref[8:8 + tm, half] = _dot(hn, wup_ref[:, cols])
            carry_ref[:, cols] = hx_ref[tm:tm + 8, half]

    for ci in range(FF_WINDOWS - 1):
        up_proj(ci)
    for ci in range(n_chunks):
        if ci + FF_WINDOWS - 1 < n_chunks:
            up_proj(ci + FF_WINDOWS - 1)
        hx_ref = hx_refs.at[ci % FF_WINDOWS]
        cw = jnp.concatenate([cw_ref[:, cols] for cols in chunk_cols(ci)], axis=-1)
        y = cw[FFN_CONV - 1:FFN_CONV] * hx_ref[8:8 + tm, :]
        for j in range(1, FFN_CONV):
            y = y + cw[FFN_CONV - 1 - j:FFN_CONV - j] * hx_ref[8 - j:8 - j + tm, :]
        hg = 0.5 * y[:, :FF_CHUNK]
        act = (hg * (1.0 + jnp.tanh(hg)) * y[:, FF_CHUNK:]).astype(BF16)
        acc = acc + _dot(act, wdown_ref[ci * FF_CHUNK:(ci + 1) * FF_CHUNK, :])
    o_ref[0] = _rms(h + acc, finw_ref[...])


def _ffn(x, y_da, y_gdn, w_out, ffn_norm_w, w_up, conv_w, w_down, final_norm_w, tm):
    b, t, d = x.shape
    tok_spec = lambda width: pl.BlockSpec((1, tm, width), lambda i, j: (i, j, 0))
    const = lambda shape: pl.BlockSpec(shape, lambda i, j: (0,) * len(shape),
                                       pipeline_mode=pl.Buffered(1))
    return pl.pallas_call(
        functools.partial(_ffn_kernel, tm=tm),
        grid=(b, t // tm),
        in_specs=[tok_spec(d), tok_spec(GROUP_W), tok_spec(GROUP_W), const(w_out.shape), const((1, d)),
                  const(w_up.shape), const(conv_w.shape), const(w_down.shape), const((1, d))],
        out_specs=tok_spec(d),
        out_shape=jax.ShapeDtypeStruct((b, t, d), F32),
        scratch_shapes=[pltpu.VMEM((FF_WINDOWS, 8 + tm, 2 * FF_CHUNK), F32),
                        pltpu.VMEM((8, 2 * D_FF), F32)],
        compiler_params=pltpu.CompilerParams(
            dimension_semantics=("parallel", "arbitrary"), vmem_limit_bytes=VMEM_LIMIT),
        name="outproj_ffn",
    )(x, y_da, y_gdn, w_out, ffn_norm_w, w_up, conv_w, w_down, final_norm_w)


def _permute_qk_cols(w):
    half = DA_HEAD_DIM // 2
    rows = w.shape[0]
    return w.reshape(rows, DA_HEADS, 2, 2, half).transpose(0, 1, 3, 2, 4).reshape(rows, GROUP_W)


def _rope_tables(t):
    half = DA_HEAD_DIM // 2
    inv_freq = ROPE_THETA ** (-jnp.arange(0, DA_HEAD_DIM, 2, dtype=F32) / DA_HEAD_DIM)
    ang = jnp.arange(t, dtype=F32)[:, None] * inv_freq[None, :]
    cos, sin = jnp.cos(ang), jnp.sin(ang)
    assert cos.shape == (t, half)
    return jnp.tile(cos, (1, 4)), jnp.concatenate([-sin, -sin, sin, sin], axis=-1)


def _tile(t, pref):
    while t % pref:
        pref //= 2
    return pref


def kernel(x, attn_norm_w, w_in, da_lambda_q1, da_lambda_k1, da_lambda_q2, da_lambda_k2, da_subln_w,
           gdn_conv_w, gdn_a_log, gdn_dt_bias, gdn_norm_w, w_out, ffn_norm_w, ffn_w_up, ffn_conv_w,
           ffn_w_down, final_norm_w):
    b, t, d = x.shape
    assert d == D_MODEL and t % GDN_CHUNK == 0
    l = 0
    w = w_in[l]
    nh = GDN_HEADS
    small = jnp.pad(w[:, 7 * GROUP_W:], ((0, 0), (0, HEAD_W - 2 * nh)))
    w_all = jnp.concatenate([_permute_qk_cols(w[:, :GROUP_W]), _permute_qk_cols(w[:, GROUP_W:2 * GROUP_W]),
                             w[:, 2 * GROUP_W:7 * GROUP_W], small], axis=-1).astype(BF16)
    cos_t, sin_t = _rope_tables(t)

    q, k, v, gq, gk, gv, gz, gba = _inproj(x, attn_norm_w[l][None, :], w_all, cos_t, sin_t, gdn_conv_w[l],
                                           _tile(t, 512))

    lam_p = jnp.stack([da_lambda_q1[l], da_lambda_k1[l], da_lambda_q2[l], da_lambda_k2[l]])
    y_da = _attention(q, k, v, lam_p, da_subln_w[l][None, :], _tile(t, 512))

    lane_pad = lambda vec: jnp.pad(vec, (nh, HEAD_W - 2 * nh))[None, :]
    y_gdn = _gdn(gq, gk, gv, gz, gba, lane_pad(gdn_a_log[l]), lane_pad(gdn_dt_bias[l]),
                 gdn_norm_w[l][None, :], _tile(t, 512), GDN_CHUNK)

    return _ffn(x, y_da, y_gdn, w_out[l].astype(BF16), ffn_norm_w[l][None, :],
                ffn_w_up[l].astype(BF16), ffn_conv_w[l],
                ffn_w_down[l].astype(BF16), final_norm_w[None, :], _tile(t, 512))
```

```python
import functools
import math

import jax
import jax.numpy as jnp
from jax import lax
from jax.experimental import pallas as pl
from jax.experimental.pallas import tpu as pltpu

F32 = jnp.float32
BF16 = jnp.bfloat16

D_MODEL = 1024
DA_HEADS = 4
DA_HEAD_DIM = 64
HEAD_W = 128
GROUP_W = 512
GDN_HEADS = 4
SHORT_CONV = 4
D_FF = 2816
FFN_CONV = 3
FF_CHUNK = 256
FF_WINDOWS = 6
ROPE_THETA = 10000.0
EPS = 1e-6
LAMBDA_INIT = 0.8 - 0.6 * math.exp(-0.3 * 0)
GDN_CHUNK = 128
INV_BLOCK = 16
GDN_GROUP = 4
NEG_BIG = -1e30
LOG2E = math.log2(math.e)
VMEM_LIMIT = 56 * 1024 * 1024


def _rms(x, w):
    return x * lax.rsqrt(jnp.mean(x * x, axis=-1, keepdims=True) + EPS) * w


def _dot(a, b):
    return jnp.dot(a, b, preferred_element_type=F32)


def _dot_nt(a, b):
    return lax.dot_general(a, b, (((1,), (1,)), ((), ())), preferred_element_type=F32)


def _sigmoid(x):
    return 1.0 / (1.0 + jnp.exp(-x))


def _silu(x):
    hx = 0.5 * x
    return hx * (1.0 + jnp.tanh(hx))


def _inproj_kernel(x_ref, nw_ref, w_ref, cos_ref, sin_ref, cw_ref,
                   q_ref, k_ref, v_ref, gq_ref, gk_ref, gv_ref, gz_ref, gba_ref,
                   win_refs, carry_ref, *, tm):
    @pl.when(pl.program_id(1) == 0)
    def _():
        carry_ref[...] = jnp.zeros_like(carry_ref)

    xn = _rms(x_ref[0], nw_ref[...]).astype(BF16)
    cos = cos_ref[...]
    sin = sin_ref[...]

    def proj(seg, width=GROUP_W):
        return _dot(xn, w_ref[:, seg * GROUP_W: seg * GROUP_W + width])

    def rope(p, scale):
        outs = []
        for h in range(DA_HEADS):
            ph = p[:, h * HEAD_W:(h + 1) * HEAD_W]
            outs.append((ph * cos + pltpu.roll(ph, HEAD_W // 2, 1) * sin) * scale)
        return jnp.concatenate(outs, axis=-1)

    q_ref[0] = rope(proj(0), DA_HEAD_DIM ** -0.5 * LOG2E).astype(BF16)
    k_ref[0] = rope(proj(1), 1.0).astype(BF16)
    v_ref[0] = proj(2).astype(BF16)

    def short_conv_silu(seg, slot, out_ref, l2_scale):
        cols = slice(slot * GROUP_W, (slot + 1) * GROUP_W)
        win = win_refs.at[slot]
        win[0:8, :] = carry_ref[:, cols]
        win[8:8 + tm, :] = proj(seg)
        carry_ref[:, cols] = win[tm:tm + 8, :]
        for h in range(GDN_HEADS):
            hc = slice(h * HEAD_W, (h + 1) * HEAD_W)
            cw = cw_ref[:, slot * GROUP_W + h * HEAD_W: slot * GROUP_W + (h + 1) * HEAD_W]
            y = cw[SHORT_CONV - 1:SHORT_CONV] * win[8:8 + tm, hc]
            for j in range(1, SHORT_CONV):
                y = y + cw[SHORT_CONV - 1 - j:SHORT_CONV - j] * win[8 - j:8 - j + tm, hc]
            y = _silu(y)
            if l2_scale is not None:
                y = y * (lax.rsqrt(jnp.sum(y * y, axis=-1, keepdims=True) + EPS) * l2_scale)
            out_ref[0, :, hc] = y.astype(BF16)

    short_conv_silu(3, 0, gq_ref, HEAD_W ** -0.5)
    short_conv_silu(4, 1, gk_ref, 1.0)
    short_conv_silu(5, 2, gv_ref, None)
    gz_ref[0] = proj(6).astype(BF16)
    gba_ref[0] = proj(7, HEAD_W)


def _inproj(x, norm_w, w_all, cos_t, sin_t, conv_w, tm):
    b, t, d = x.shape
    grid = (b, t // tm)
    tok = lambda width, dt: jax.ShapeDtypeStruct((b, t, width), dt)
    tok_spec = lambda width: pl.BlockSpec((1, tm, width), lambda i, j: (i, j, 0))
    const = lambda shape: pl.BlockSpec(shape, lambda i, j: (0,) * len(shape),
                                       pipeline_mode=pl.Buffered(1))
    return pl.pallas_call(
        functools.partial(_inproj_kernel, tm=tm),
        grid=grid,
        in_specs=[tok_spec(d), const((1, d)), const(w_all.shape),
                  pl.BlockSpec((tm, HEAD_W), lambda i, j: (j, 0)),
                  pl.BlockSpec((tm, HEAD_W), lambda i, j: (j, 0)),
                  const(conv_w.shape)],
        out_specs=[tok_spec(GROUP_W)] * 7 + [tok_spec(HEAD_W)],
        out_shape=[tok(GROUP_W, BF16)] * 7 + [tok(HEAD_W, F32)],
        scratch_shapes=[pltpu.VMEM((3, 8 + tm, GROUP_W), F32),
                        pltpu.VMEM((8, 3 * GROUP_W), F32)],
        compiler_params=pltpu.CompilerParams(
            dimension_semantics=("parallel", "arbitrary"), vmem_limit_bytes=VMEM_LIMIT),
        name="inproj",
    )(x, norm_w, w_all, cos_t, sin_t, conv_w)


def _attn_kernel(q_ref, k_ref, v_ref, lam_ref, sw_ref, o_ref, m_ref, a_ref, *, tq, tk):
    i = pl.program_id(1)
    lane = lax.broadcasted_iota(jnp.int32, (1, HEAD_W), 1)
    first_map = (lane % 64) < 32
    head_cols = [slice(h * HEAD_W, (h + 1) * HEAD_W) for h in range(DA_HEADS)]
    streams = []
    for h in range(DA_HEADS):
        q = q_ref[0, :, head_cols[h]]
        zero = jnp.zeros_like(q)
        streams.append((h, 2 * h, jnp.where(first_map, q, zero)))
        streams.append((h, 2 * h + 1, jnp.where(first_map, zero, q)))
    def block(st, width, diag):
        for r0, nr in (((0, tq // 2), (tq // 2, tq // 2)) if diag else ((0, tq),)):
            rows = slice(r0, r0 + nr)
            ncols = width - tq + r0 + nr if diag else width
            kb = [k_ref[0, pl.ds(st, ncols), head_cols[h]] for h in range(DA_HEADS)]
            ones = jnp.ones((ncols, HEAD_W), BF16)
            vb = [jnp.concatenate([v_ref[0, pl.ds(st, ncols), head_cols[h]], ones], axis=-1) for h in range(DA_HEADS)]
            scores = [_dot_nt(qm[rows], kb[h]) for h, _, qm in streams]
            for s, (h, si, _) in zip(scores, streams):
                if diag:
                    row = lax.broadcasted_iota(jnp.int32, (nr, ncols), 0) + (r0 + width - tq)
                    col = lax.broadcasted_iota(jnp.int32, (nr, ncols), 1)
                    s = jnp.where(col <= row, s, NEG_BIG)
                    m_new = jnp.broadcast_to(jnp.max(s, axis=-1, keepdims=True), (nr, HEAD_W))
                else:
                    m_old = m_ref[si, rows]
                    m_new = jnp.maximum(m_old, jnp.max(s, axis=-1, keepdims=True))
                p = jnp.concatenate([jnp.exp2((s[:, c * HEAD_W:(c + 1) * HEAD_W] - m_new).astype(BF16))
                                     for c in range(ncols // HEAD_W)], axis=-1)
                pv = _dot(p, vb[h])
                if diag:
                    a_ref[si, rows] = pv
                else:
                    alpha = jnp.exp2(m_old - m_new)
                    a_ref[si, rows] = jnp.concatenate([alpha, alpha], axis=-1) * a_ref[si, rows] + pv
                m_ref[si, rows] = m_new

    @pl.when(i % 2 == 1)
    def _():
        block(pl.multiple_of((i - 1) * tq, tk), tk, True)

    @pl.when(i % 2 == 0)
    def _():
        block(pl.multiple_of(i * tq, tk), tq, True)

    def body(j, carry):
        block(pl.multiple_of(j * tk, tk), tk, False)
        return carry

    lax.fori_loop(0, (i * tq) // tk, body, 0)

    lp = lam_ref[...]
    lam = (jnp.exp(jnp.sum(lp[0:1] * lp[1:2], axis=-1, keepdims=True))
           - jnp.exp(jnp.sum(lp[2:3] * lp[3:4], axis=-1, keepdims=True)) + LAMBDA_INIT)
    for h in range(DA_HEADS):
        a1 = a_ref[2 * h]
        a2 = a_ref[2 * h + 1]
        o = a1[:, :HEAD_W] * (1.0 / a1[:, HEAD_W:]) - lam * (a2[:, :HEAD_W] * (1.0 / a2[:, HEAD_W:]))
        o_ref[0, :, head_cols[h]] = (_rms(o, sw_ref[...]) * (1.0 - LAMBDA_INIT)).astype(BF16)


def _attention(q, k, v, lam_p, subln_w, tq):
    b, t, _ = q.shape
    tk = 2 * tq
    assert t % tq == 0 and tq % (2 * HEAD_W) == 0
    n_streams = 2 * DA_HEADS
    return pl.pallas_call(
        functools.partial(_attn_kernel, tq=tq, tk=tk),
        grid=(b, t // tq),
        in_specs=[pl.BlockSpec((1, tq, GROUP_W), lambda bi, i: (bi, i, 0)),
                  pl.BlockSpec((1, t, GROUP_W), lambda bi, i: (bi, 0, 0)),
                  pl.BlockSpec((1, t, GROUP_W), lambda bi, i: (bi, 0, 0)),
                  pl.BlockSpec(lam_p.shape, lambda bi, i: (0, 0)),
                  pl.BlockSpec((1, HEAD_W), lambda bi, i: (0, 0))],
        out_specs=pl.BlockSpec((1, tq, GROUP_W), lambda bi, i: (bi, i, 0)),
        out_shape=jax.ShapeDtypeStruct((b, t, GROUP_W), BF16),
        scratch_shapes=[pltpu.VMEM((n_streams, tq, HEAD_W), F32),
                        pltpu.VMEM((n_streams, tq, 2 * HEAD_W), F32)],
        compiler_params=pltpu.CompilerParams(
            dimension_semantics=("parallel", "arbitrary"), vmem_limit_bytes=VMEM_LIMIT),
        name="diff_attn",
    )(q, k, v, lam_p, subln_w)


def _gdn_kernel(gq_ref, gk_ref, gv_ref, gz_ref, gba_ref, alog_ref, dtb_ref, nw_ref, o_ref,
                s_ref, u_ref, w_ref, qk_ref, qd_ref, kdt_ref, gl_ref, *, tt, c):
    nh = GDN_HEADS
    nc = tt // c

    @pl.when(pl.program_id(1) == 0)
    def _():
        s_ref[...] = jnp.zeros_like(s_ref)

    gba = gba_ref[0]
    beta_all = _sigmoid(gba)
    xa = gba + dtb_ref[...]
    g_all = -jnp.exp(alog_ref[...]) * (jnp.maximum(xa, 0.0) + jnp.log1p(jnp.exp(-jnp.abs(xa))))

    r_i = lax.broadcasted_iota(jnp.int32, (c, c), 0)
    c_i = lax.broadcasted_iota(jnp.int32, (c, c), 1)
    tril = c_i <= r_i
    strict = c_i < r_i
    eye = (c_i == r_i).astype(F32)
    tril_f = tril.astype(F32)
    blk_mask = [(r_i // INV_BLOCK) == (c_i // INV_BLOCK)]
    size = INV_BLOCK
    while size < c:
        blk_mask.append(((r_i // (2 * size)) == (c_i // (2 * size))) & ((r_i // size) > (c_i // size)))
        size *= 2

    def level(fn, items):
        return [fn(x) for x in items]

    for g0 in range(0, nc, GDN_GROUP):
        prep = []
        for ci in range(g0, g0 + GDN_GROUP):
            rows = slice(ci * c, (ci + 1) * c)
            gcum = jnp.dot(tril_f, g_all[rows], preferred_element_type=F32, precision=lax.Precision.HIGHEST)
            gcum_t = gcum.T
            for h in range(nh):
                cols = slice(h * HEAD_W, (h + 1) * HEAD_W)
                q_bf = gq_ref[0, rows, cols]
                k_bf = gk_ref[0, rows, cols]
                q = q_bf.astype(F32)
                k = k_bf.astype(F32)
                v = gv_ref[0, rows, cols].astype(F32)
                beta = beta_all[rows, h:h + 1]
                gc = gcum[:, nh + h:nh + h + 1]
                gr = gcum_t[nh + h:nh + h + 1, :]
                g_last = gcum[c - 1:c, nh + h:nh + h + 1]
                decay = jnp.exp(jnp.where(tril, gc - gr, -jnp.inf))
                e_gc = jnp.exp(gc)
                kb = k * beta
                idx = ci * nh + h
                qd_ref[idx] = (q * e_gc).astype(BF16)
                kdt_ref[idx] = (k * jnp.exp(g_last - gc)).T.astype(BF16)
                gl_ref[idx] = jnp.broadcast_to(jnp.exp(g_last), (8, HEAD_W))
                rhs = jnp.concatenate([v * beta, kb * e_gc], axis=-1).astype(BF16)
                prep.append((idx, q_bf, k_bf, kb.astype(BF16), decay, rhs))
        kkt = level(lambda a: _dot_nt(a[3], a[2]) * a[4], prep)
        for a, qk in zip(prep, level(lambda a: _dot_nt(a[1], a[2]) * a[4], prep)):
            qk_ref[a[0]] = qk.astype(BF16)
        lmat = [jnp.where(strict, x, 0.0) for x in kkt]
        p = [jnp.where(blk_mask[0], -x, 0.0) for x in lmat]
        t_inv = [eye + x for x in p]
        for _ in range(int(math.log2(INV_BLOCK)) - 1):
            p = level(lambda x: _dot(x.astype(BF16), x.astype(BF16)), p)
            t_inv = level(lambda tx: tx[0] + _dot(tx[0].astype(BF16), tx[1].astype(BF16)), list(zip(t_inv, p)))
        for off_mask in blk_mask[1:]:
            t_bf = [x.astype(BF16) for x in t_inv]
            x_off = level(lambda tl: _dot(tl[0], jnp.where(off_mask, tl[1], 0.0).astype(BF16)).astype(BF16),
                          list(zip(t_bf, lmat)))
            t_inv = level(lambda txb: txb[0] - _dot(txb[1], txb[2]), list(zip(t_inv, x_off, t_bf)))
        for a, sol in zip(prep, level(lambda ta: _dot(ta[0].astype(BF16), ta[1][5]), list(zip(t_inv, prep)))):
            u_ref[a[0]] = sol[:, :HEAD_W]
            w_ref[a[0]] = sol[:, HEAD_W:].astype(BF16)

    nw = nw_ref[...]
    heads = range(nh)
    for ci in range(nc):
        rows = slice(ci * c, (ci + 1) * c)
        s_old = [s_ref[h] for h in heads]
        s_bf = [x.astype(BF16) for x in s_old]
        ws = [_dot(w_ref[ci * nh + h], s_bf[h]) for h in heads]
        qs = [_dot(qd_ref[ci * nh + h], s_bf[h]) for h in heads]
        v_bf = [(u_ref[ci * nh + h] - ws[h]).astype(BF16) for h in heads]
        kv = [_dot(kdt_ref[ci * nh + h], v_bf[h]) for h in heads]
        for h in heads:
            s_ref[h] = s_old[h] * gl_ref[ci * nh + h][0:1, :] + kv[h]
        o = [qs[h] + _dot(qk_ref[ci * nh + h], v_bf[h]) for h in heads]
        for h in heads:
            cols = slice(h * HEAD_W, (h + 1) * HEAD_W)
            z = gz_ref[0, rows, cols].astype(F32)
            o_ref[0, rows, cols] = (_rms(o[h], nw) * _silu(z)).astype(BF16)


def _gdn(gq, gk, gv, gz, gba, alog_row, dtb_row, norm_w, tt, c):
    b, t, _ = gq.shape
    nidx = (tt // c) * GDN_HEADS
    tok_spec = lambda width: pl.BlockSpec((1, tt, width), lambda i, j: (i, j, 0))
    const = lambda shape: pl.BlockSpec(shape, lambda i, j: (0,) * len(shape))
    return pl.pallas_call(
        functools.partial(_gdn_kernel, tt=tt, c=c),
        grid=(b, t // tt),
        in_specs=[tok_spec(GROUP_W)] * 4 + [tok_spec(HEAD_W)] + [const((1, HEAD_W))] * 3,
        out_specs=tok_spec(GROUP_W),
        out_shape=jax.ShapeDtypeStruct((b, t, GROUP_W), BF16),
        scratch_shapes=[pltpu.VMEM((GDN_HEADS, HEAD_W, HEAD_W), F32),
                        pltpu.VMEM((nidx, c, HEAD_W), F32),
                        pltpu.VMEM((nidx, c, HEAD_W), BF16),
                        pltpu.VMEM((nidx, c, c), BF16),
                        pltpu.VMEM((nidx, c, HEAD_W), BF16),
                        pltpu.VMEM((nidx, HEAD_W, c), BF16),
                        pltpu.VMEM((nidx, 8, HEAD_W), F32)],
        compiler_params=pltpu.CompilerParams(
            dimension_semantics=("parallel", "arbitrary"), vmem_limit_bytes=VMEM_LIMIT),
        name="gdn",
    )(gq, gk, gv, gz, gba, alog_row, dtb_row, norm_w)


def _ffn_kernel(x_ref, yda_ref, ygdn_ref, wout_ref, fnw_ref, wup_ref, cw_ref, wdown_ref, finw_ref, o_ref,
                hx_refs, carry_ref, *, tm):
    n_chunks = D_FF // FF_CHUNK
    cw2 = 2 * FF_CHUNK

    @pl.when(pl.program_id(1) == 0)
    def _():
        carry_ref[...] = jnp.zeros_like(carry_ref)

    h = (x_ref[0] + _dot(yda_ref[0], wout_ref[0:GROUP_W, :])
         + _dot(ygdn_ref[0], wout_ref[GROUP_W:2 * GROUP_W, :]))
    hn = _rms(h, fnw_ref[...]).astype(BF16)
    acc = jnp.zeros((tm, D_MODEL), F32)

    halves = (slice(0, FF_CHUNK), slice(FF_CHUNK, cw2))

    def chunk_cols(ci):
        return (slice(ci * FF_CHUNK, (ci + 1) * FF_CHUNK), slice(D_FF + ci * FF_CHUNK, D_FF + (ci + 1) * FF_CHUNK))

    def up_proj(ci):
        hx_ref = hx_refs.at[ci % FF_WINDOWS]
        for half, cols in zip(halves, chunk_cols(ci)):
            hx_ref[0:8, half] = carry_ref[:, cols]
            hx_ref[8:8 + tm, half] = _dot(hn, wup_ref[:, cols])
            carry_ref[:, cols] = hx_ref[tm:tm + 8, half]

    for ci in range(FF_WINDOWS - 1):
        up_proj(ci)
    for ci in range(n_chunks):
        if ci + FF_WINDOWS - 1 < n_chunks:
            up_proj(ci + FF_WINDOWS - 1)
        hx_ref = hx_refs.at[ci % FF_WINDOWS]
        cw = jnp.concatenate([cw_ref[:, cols] for cols in chunk_cols(ci)], axis=-1)
        y = cw[FFN_CONV - 1:FFN_CONV] * hx_ref[8:8 + tm, :]
        for j in range(1, FFN_CONV):
            y = y + cw[FFN_CONV - 1 - j:FFN_CONV - j] * hx_ref[8 - j:8 - j + tm, :]
        hg = 0.5 * y[:, :FF_CHUNK]
        act = (hg * (1.0 + jnp.tanh(hg)) * y[:, FF_CHUNK:]).astype(BF16)
        acc = acc + _dot(act, wdown_ref[ci * FF_CHUNK:(ci + 1) * FF_CHUNK, :])
    o_ref[0] = _rms(h + acc, finw_ref[...])


def _ffn(x, y_da, y_gdn, w_out, ffn_norm_w, w_up, conv_w, w_down, final_norm_w, tm):
    b, t, d = x.shape
    tok_spec = lambda width: pl.BlockSpec((1, tm, width), lambda i, j: (i, j, 0))
    const = lambda shape: pl.BlockSpec(shape, lambda i, j: (0,) * len(shape),
                                       pipeline_mode=pl.Buffered(1))
    return pl.pallas_call(
        functools.partial(_ffn_kernel, tm=tm),
        grid=(b, t // tm),
        in_specs=[tok_spec(d), tok_spec(GROUP_W), tok_spec(GROUP_W), const(w_out.shape), const((1, d)),
                  const(w_up.shape), const(conv_w.shape), const(w_down.shape), const((1, d))],
        out_specs=tok_spec(d),
        out_shape=jax.ShapeDtypeStruct((b, t, d), F32),
        scratch_shapes=[pltpu.VMEM((FF_WINDOWS, 8 + tm, 2 * FF_CHUNK), F32),
                        pltpu.VMEM((8, 2 * D_FF), F32)],
        compiler_params=pltpu.CompilerParams(
            dimension_semantics=("parallel", "arbitrary"), vmem_limit_bytes=VMEM_LIMIT),
        name="outproj_ffn",
    )(x, y_da, y_gdn, w_out, ffn_norm_w, w_up, conv_w, w_down, final_norm_w)


def _permute_qk_cols(w):
    half = DA_HEAD_DIM // 2
    rows = w.shape[0]
    return w.reshape(rows, DA_HEADS, 2, 2, half).transpose(0, 1, 3, 2, 4).reshape(rows, GROUP_W)


def _rope_tables(t):
    half = DA_HEAD_DIM // 2
    inv_freq = ROPE_THETA ** (-jnp.arange(0, DA_HEAD_DIM, 2, dtype=F32) / DA_HEAD_DIM)
    ang = jnp.arange(t, dtype=F32)[:, None] * inv_freq[None, :]
    cos, sin = jnp.cos(ang), jnp.sin(ang)
    assert cos.shape == (t, half)
    return jnp.tile(cos, (1, 4)), jnp.concatenate([-sin, -sin, sin, sin], axis=-1)


def _tile(t, pref):
    while t % pref:
        pref //= 2
    return pref


def kernel(x, attn_norm_w, w_in, da_lambda_q1, da_lambda_k1, da_lambda_q2, da_lambda_k2, da_subln_w,
           gdn_conv_w, gdn_a_log, gdn_dt_bias, gdn_norm_w, w_out, ffn_norm_w, ffn_w_up, ffn_conv_w,
           ffn_w_down, final_norm_w):
    b, t, d = x.shape
    assert d == D_MODEL and t % GDN_CHUNK == 0
    l = 0
    w = w_in[l]
    nh = GDN_HEADS
    small = jnp.pad(w[:, 7 * GROUP_W:], ((0, 0), (0, HEAD_W - 2 * nh)))
    w_all = jnp.concatenate([_permute_qk_cols(w[:, :GROUP_W]), _permute_qk_cols(w[:, GROUP_W:2 * GROUP_W]),
                             w[:, 2 * GROUP_W:7 * GROUP_W], small], axis=-1).astype(BF16)
    cos_t, sin_t = _rope_tables(t)

    q, k, v, gq, gk, gv, gz, gba = _inproj(x, attn_norm_w[l][None, :], w_all, cos_t, sin_t, gdn_conv_w[l],
                                           _tile(t, 512))

    lam_p = jnp.stack([da_lambda_q1[l], da_lambda_k1[l], da_lambda_q2[l], da_lambda_k2[l]])
    y_da = _attention(q, k, v, lam_p, da_subln_w[l][None, :], _tile(t, 512))

    lane_pad = lambda vec: jnp.pad(vec, (nh, HEAD_W - 2 * nh))[None, :]
    y_gdn = _gdn(gq, gk, gv, gz, gba, lane_pad(gdn_a_log[l]), lane_pad(gdn_dt_bias[l]),
                 gdn_norm_w[l][None, :], _tile(t, 512), GDN_CHUNK)

    return _ffn(x, y_da, y_gdn, w_out[l].astype(BF16), ffn_norm_w[l][None, :],
                ffn_w_up[l].astype(BF16), ffn_conv_w[l],
                ffn_w_down[l].astype(BF16), final_norm_w[None, :], _tile(t, 512))
```

```python
import functools
import math

import jax
import jax.numpy as jnp
from jax import lax
from jax.experimental import pallas as pl
from jax.experimental.pallas import tpu as pltpu

F32 = jnp.float32
BF16 = jnp.bfloat16

D_MODEL = 1024
DA_HEADS = 4
DA_HEAD_DIM = 64
HEAD_W = 128
GROUP_W = 512
GDN_HEADS = 4
SHORT_CONV = 4
D_FF = 2816
FFN_CONV = 3
FF_CHUNK = 256
FF_WINDOWS = 6
ROPE_THETA = 10000.0
EPS = 1e-6
LAMBDA_INIT = 0.8 - 0.6 * math.exp(-0.3 * 0)
GDN_CHUNK = 128
INV_BLOCK = 16
GDN_GROUP = 4
NEG_BIG = -1e30
LOG2E = math.log2(math.e)
VMEM_LIMIT = 56 * 1024 * 1024


def _rms(x, w):
    return x * lax.rsqrt(jnp.mean(x * x, axis=-1, keepdims=True) + EPS) * w


def _dot(a, b):
    return jnp.dot(a, b, preferred_element_type=F32)


def _dot_nt(a, b):
    return lax.dot_general(a, b, (((1,), (1,)), ((), ())), preferred_element_type=F32)


def _sigmoid(x):
    return 1.0 / (1.0 + jnp.exp(-x))


def _silu(x):
    hx = 0.5 * x
    return hx * (1.0 + jnp.tanh(hx))


def _inproj_kernel(x_ref, nw_ref, w_ref, cos_ref, sin_ref, cw_ref,
                   q_ref, k_ref, v_ref, gq_ref, gk_ref, gv_ref, gz_ref, gba_ref,
                   win0_ref, win1_ref, win2_ref, carry_ref, *, tm):
    @pl.when(pl.program_id(1) == 0)
    def _():
        carry_ref[...] = jnp.zeros_like(carry_ref)

    xn = _rms(x_ref[0], nw_ref[...]).astype(BF16)
    cos = cos_ref[...]
    sin = sin_ref[...]

    def proj(seg, width=GROUP_W):
        return _dot(xn, w_ref[:, seg * GROUP_W: seg * GROUP_W + width])

    def rope(p, scale):
        outs = []
        for h in range(DA_HEADS):
            ph = p[:, h * HEAD_W:(h + 1) * HEAD_W]
            outs.append((ph * cos + pltpu.roll(ph, HEAD_W // 2, 1) * sin) * scale)
        return jnp.concatenate(outs, axis=-1)

    wins = (win0_ref, win1_ref, win2_ref)

    def fill_window(seg, slot):
        cols = slice(slot * GROUP_W, (slot + 1) * GROUP_W)
        win = wins[slot]
        win[0:8, :] = carry_ref[:, cols]
        win[8:8 + tm, :] = proj(seg)
        carry_ref[:, cols] = win[tm:tm + 8, :]

    def short_conv_silu(slot, out_ref, l2_scale):
        win = wins[slot]
        for h in range(GDN_HEADS):
            hc = slice(h * HEAD_W, (h + 1) * HEAD_W)
            cw = cw_ref[:, slot * GROUP_W + h * HEAD_W: slot * GROUP_W + (h + 1) * HEAD_W]
            y = cw[SHORT_CONV - 1:SHORT_CONV] * win[8:8 + tm, hc]
            for j in range(1, SHORT_CONV):
                y = y + cw[SHORT_CONV - 1 - j:SHORT_CONV - j] * win[8 - j:8 - j + tm, hc]
            y = _silu(y)
            if l2_scale is not None:
                y = y * (lax.rsqrt(jnp.sum(y * y, axis=-1, keepdims=True) + EPS) * l2_scale)
            out_ref[0, :, hc] = y.astype(BF16)

    fill_window(3, 0)
    v_ref[0] = proj(2).astype(BF16)
    fill_window(4, 1)
    q_ref[0] = rope(proj(0), DA_HEAD_DIM ** -0.5 * LOG2E).astype(BF16)
    short_conv_silu(0, gq_ref, HEAD_W ** -0.5)
    fill_window(5, 2)
    gz_ref[0] = proj(6).astype(BF16)
    k_ref[0] = rope(proj(1), 1.0).astype(BF16)
    short_conv_silu(1, gk_ref, 1.0)
    gba_ref[0] = proj(7, HEAD_W)
    short_conv_silu(2, gv_ref, None)


def _inproj(x, norm_w, w_all, cos_t, sin_t, conv_w, tm):
    b, t, d = x.shape
    grid = (b, t // tm)
    tok = lambda width, dt: jax.ShapeDtypeStruct((b, t, width), dt)
    tok_spec = lambda width: pl.BlockSpec((1, tm, width), lambda i, j: (i, j, 0))
    const = lambda shape: pl.BlockSpec(shape, lambda i, j: (0,) * len(shape),
                                       pipeline_mode=pl.Buffered(1))
    return pl.pallas_call(
        functools.partial(_inproj_kernel, tm=tm),
        grid=grid,
        in_specs=[tok_spec(d), const((1, d)), const(w_all.shape),
                  pl.BlockSpec((tm, HEAD_W), lambda i, j: (j, 0)),
                  pl.BlockSpec((tm, HEAD_W), lambda i, j: (j, 0)),
                  const(conv_w.shape)],
        out_specs=[tok_spec(GROUP_W)] * 7 + [tok_spec(HEAD_W)],
        out_shape=[tok(GROUP_W, BF16)] * 7 + [tok(HEAD_W, F32)],
        scratch_shapes=[pltpu.VMEM((8 + tm, GROUP_W), F32)] * 3 + [
                        pltpu.VMEM((8, 3 * GROUP_W), F32)],
        compiler_params=pltpu.CompilerParams(
            dimension_semantics=("parallel", "arbitrary"), vmem_limit_bytes=VMEM_LIMIT),
        name="inproj",
    )(x, norm_w, w_all, cos_t, sin_t, conv_w)


def _attn_kernel(q_ref, k_ref, v_ref, lam_ref, sw_ref, o_ref, m_ref, a_ref, *, tq, tk):
    i = pl.program_id(1)
    lane = lax.broadcasted_iota(jnp.int32, (1, HEAD_W), 1)
    first_map = (lane % 64) < 32
    head_cols = [slice(h * HEAD_W, (h + 1) * HEAD_W) for h in range(DA_HEADS)]
    streams = []
    for h in range(DA_HEADS):
        q = q_ref[0, :, head_cols[h]]
        zero = jnp.zeros_like(q)
        streams.append((h, 2 * h, jnp.where(first_map, q, zero)))
        streams.append((h, 2 * h + 1, jnp.where(first_map, zero, q)))
    def block(st, width, diag):
        for r0, nr in (((0, tq // 2), (tq // 2, tq // 2)) if diag else ((0, tq),)):
            rows = slice(r0, r0 + nr)
            ncols = width - tq + r0 + nr if diag else width
            kb = [k_ref[0, pl.ds(st, ncols), head_cols[h]] for h in range(DA_HEADS)]
            ones = jnp.ones((ncols, HEAD_W), BF16)
            vb = [jnp.concatenate([v_ref[0, pl.ds(st, ncols), head_cols[h]], ones], axis=-1) for h in range(DA_HEADS)]
            scores = [_dot_nt(qm[rows], kb[h]) for h, _, qm in streams]
            for s, (h, si, _) in zip(scores, streams):
                if diag:
                    row = lax.broadcasted_iota(jnp.int32, (nr, ncols), 0) + (r0 + width - tq)
                    col = lax.broadcasted_iota(jnp.int32, (nr, ncols), 1)
                    s = jnp.where(col <= row, s, NEG_BIG)
                    m_new = jnp.broadcast_to(jnp.max(s, axis=-1, keepdims=True), (nr, HEAD_W))
                else:
                    m_old = m_ref[si, rows]
                    m_new = jnp.maximum(m_old, jnp.max(s, axis=-1, keepdims=True))
                p = jnp.concatenate([jnp.exp2((s[:, c * HEAD_W:(c + 1) * HEAD_W] - m_new).astype(BF16))
                                     for c in range(ncols // HEAD_W)], axis=-1)
                pv = _dot(p, vb[h])
                if diag:
                    a_ref[si, rows] = pv
                else:
                    alpha = jnp.exp2(m_old - m_new)
                    a_ref[si, rows] = jnp.concatenate([alpha, alpha], axis=-1) * a_ref[si, rows] + pv
                m_ref[si, rows] = m_new

    @pl.when(i % 2 == 1)
    def _():
        block(pl.multiple_of((i - 1) * tq, tk), tk, True)

    @pl.when(i % 2 == 0)
    def _():
        block(pl.multiple_of(i * tq, tk), tq, True)

    def body(j, carry):
        block(pl.multiple_of(j * tk, tk), tk, False)
        return carry

    lax.fori_loop(0, (i * tq) // tk, body, 0)

    lp = lam_ref[...]
    lam = (jnp.exp(jnp.sum(lp[0:1] * lp[1:2], axis=-1, keepdims=True))
           - jnp.exp(jnp.sum(lp[2:3] * lp[3:4], axis=-1, keepdims=True)) + LAMBDA_INIT)
    for h in range(DA_HEADS):
        a1 = a_ref[2 * h]
        a2 = a_ref[2 * h + 1]
        o = a1[:, :HEAD_W] * (1.0 / a1[:, HEAD_W:]) - lam * (a2[:, :HEAD_W] * (1.0 / a2[:, HEAD_W:]))
        o_ref[0, :, head_cols[h]] = (_rms(o, sw_ref[...]) * (1.0 - LAMBDA_INIT)).astype(BF16)


def _attention(q, k, v, lam_p, subln_w, tq):
    b, t, _ = q.shape
    tk = 2 * tq
    assert t % tq == 0 and tq % (2 * HEAD_W) == 0
    n_streams = 2 * DA_HEADS
    return pl.pallas_call(
        functools.partial(_attn_kernel, tq=tq, tk=tk),
        grid=(b, t // tq),
        in_specs=[pl.BlockSpec((1, tq, GROUP_W), lambda bi, i: (bi, i, 0)),
                  pl.BlockSpec((1, t, GROUP_W), lambda bi, i: (bi, 0, 0)),
                  pl.BlockSpec((1, t, GROUP_W), lambda bi, i: (bi, 0, 0)),
                  pl.BlockSpec(lam_p.shape, lambda bi, i: (0, 0)),
                  pl.BlockSpec((1, HEAD_W), lambda bi, i: (0, 0))],
        out_specs=pl.BlockSpec((1, tq, GROUP_W), lambda bi, i: (bi, i, 0)),
        out_shape=jax.ShapeDtypeStruct((b, t, GROUP_W), BF16),
        scratch_shapes=[pltpu.VMEM((n_streams, tq, HEAD_W), F32),
                        pltpu.VMEM((n_streams, tq, 2 * HEAD_W), F32)],
        compiler_params=pltpu.CompilerParams(
            dimension_semantics=("parallel", "arbitrary"), vmem_limit_bytes=VMEM_LIMIT),
        name="diff_attn",
    )(q, k, v, lam_p, subln_w)


def _gdn_kernel(gq_ref, gk_ref, gv_ref, gz_ref, gba_ref, alog_ref, dtb_ref, nw_ref, o_ref,
                s_ref, u_ref, w_ref, qk_ref, qd_ref, kdt_ref, gl_ref, *, tt, c):
    nh = GDN_HEADS
    nc = tt // c

    @pl.when(pl.program_id(1) == 0)
    def _():
        s_ref[...] = jnp.zeros_like(s_ref)

    gba = gba_ref[0]
    beta_all = _sigmoid(gba)
    xa = gba + dtb_ref[...]
    g_all = -jnp.exp(alog_ref[...]) * (jnp.maximum(xa, 0.0) + jnp.log1p(jnp.exp(-jnp.abs(xa))))

    r_i = lax.broadcasted_iota(jnp.int32, (c, c), 0)
    c_i = lax.broadcasted_iota(jnp.int32, (c, c), 1)
    tril = c_i <= r_i
    strict = c_i < r_i
    eye = (c_i == r_i).astype(F32)
    tril_f = tril.astype(F32)
    blk_mask = [(r_i // INV_BLOCK) == (c_i // INV_BLOCK)]
    size = INV_BLOCK
    while size < c:
        blk_mask.append(((r_i // (2 * size)) == (c_i // (2 * size))) & ((r_i // size) > (c_i // size)))
        size *= 2

    def level(fn, items):
        return [fn(x) for x in items]

    for g0 in range(0, nc, GDN_GROUP):
        prep = []
        for ci in range(g0, g0 + GDN_GROUP):
            rows = slice(ci * c, (ci + 1) * c)
            gcum = jnp.dot(tril_f, g_all[rows], preferred_element_type=F32, precision=lax.Precision.HIGHEST)
            gcum_t = gcum.T
            for h in range(nh):
                cols = slice(h * HEAD_W, (h + 1) * HEAD_W)
                q_bf = gq_ref[0, rows, cols]
                k_bf = gk_ref[0, rows, cols]
                q = q_bf.astype(F32)
                k = k_bf.astype(F32)
                v = gv_ref[0, rows, cols].astype(F32)
                beta = beta_all[rows, h:h + 1]
                gc = gcum[:, nh + h:nh + h + 1]
                gr = gcum_t[nh + h:nh + h + 1, :]
                g_last = gcum[c - 1:c, nh + h:nh + h + 1]
                decay = jnp.exp(jnp.where(tril, gc - gr, -jnp.inf))
                e_gc = jnp.exp(gc)
                kb = k * beta
                idx = ci * nh + h
                qd_ref[idx] = (q * e_gc).astype(BF16)
                kdt_ref[idx] = (k * jnp.exp(g_last - gc)).T.astype(BF16)
                gl_ref[idx] = jnp.broadcast_to(jnp.exp(g_last), (8, HEAD_W))
                rhs = jnp.concatenate([v * beta, kb * e_gc], axis=-1).astype(BF16)
                prep.append((idx, q_bf, k_bf, kb.astype(BF16), decay, rhs))
        kkt = level(lambda a: _dot_nt(a[3], a[2]) * a[4], prep)
        for a, qk in zip(prep, level(lambda a: _dot_nt(a[1], a[2]) * a[4], prep)):
            qk_ref[a[0]] = qk.astype(BF16)
        lmat = [jnp.where(strict, x, 0.0) for x in kkt]
        p = [jnp.where(blk_mask[0], -x, 0.0) for x in lmat]
        t_inv = [eye + x for x in p]
        for _ in range(int(math.log2(INV_BLOCK)) - 1):
            p = level(lambda x: _dot(x.astype(BF16), x.astype(BF16)), p)
            t_inv = level(lambda tx: tx[0] + _dot(tx[0].astype(BF16), tx[1].astype(BF16)), list(zip(t_inv, p)))
        for off_mask in blk_mask[1:]:
            t_bf = [x.astype(BF16) for x in t_inv]
            x_off = level(lambda tl: _dot(tl[0], jnp.where(off_mask, tl[1], 0.0).astype(BF16)).astype(BF16),
                          list(zip(t_bf, lmat)))
            t_inv = level(lambda txb: txb[0] - _dot(txb[1], txb[2]), list(zip(t_inv, x_off, t_bf)))
        for a, sol in zip(prep, level(lambda ta: _dot(ta[0].astype(BF16), ta[1][5]), list(zip(t_inv, prep)))):
            u_ref[a[0]] = sol[:, :HEAD_W]
            w_ref[a[0]] = sol[:, HEAD_W:].astype(BF16)

    nw = nw_ref[...]
    heads = range(nh)
    for ci in range(nc):
        rows = slice(ci * c, (ci + 1) * c)
        s_old = [s_ref[h] for h in heads]
        s_bf = [x.astype(BF16) for x in s_old]
        ws = [_dot(w_ref[ci * nh + h], s_bf[h]) for h in heads]
        qs = [_dot(qd_ref[ci * nh + h], s_bf[h]) for h in heads]
        v_bf = [(u_ref[ci * nh + h] - ws[h]).astype(BF16) for h in heads]
        kv = [_dot(kdt_ref[ci * nh + h], v_bf[h]) for h in heads]
        for h in heads:
            s_ref[h] = s_old[h] * gl_ref[ci * nh + h][0:1, :] + kv[h]
        o = [qs[h] + _dot(qk_ref[ci * nh + h], v_bf[h]) for h in heads]
        for h in heads:
            cols = slice(h * HEAD_W, (h + 1) * HEAD_W)
            z = gz_ref[0, rows, cols].astype(F32)
            o_ref[0, rows, cols] = (_rms(o[h], nw) * _silu(z)).astype(BF16)


def _gdn(gq, gk, gv, gz, gba, alog_row, dtb_row, norm_w, tt, c):
    b, t, _ = gq.shape
    nidx = (tt // c) * GDN_HEADS
    tok_spec = lambda width: pl.BlockSpec((1, tt, width), lambda i, j: (i, j, 0))
    const = lambda shape: pl.BlockSpec(shape, lambda i, j: (0,) * len(shape))
    return pl.pallas_call(
        functools.partial(_gdn_kernel, tt=tt, c=c),
        grid=(b, t // tt),
        in_specs=[tok_spec(GROUP_W)] * 4 + [tok_spec(HEAD_W)] + [const((1, HEAD_W))] * 3,
        out_specs=tok_spec(GROUP_W),
        out_shape=jax.ShapeDtypeStruct((b, t, GROUP_W), BF16),
        scratch_shapes=[pltpu.VMEM((GDN_HEADS, HEAD_W, HEAD_W), F32),
                        pltpu.VMEM((nidx, c, HEAD_W), F32),
                        pltpu.VMEM((nidx, c, HEAD_W), BF16),
                        pltpu.VMEM((nidx, c, c), BF16),
                        pltpu.VMEM((nidx, c, HEAD_W), BF16),
                        pltpu.VMEM((nidx, HEAD_W, c), BF16),
                        pltpu.VMEM((nidx, 8, HEAD_W), F32)],
        compiler_params=pltpu.CompilerParams(
            dimension_semantics=("parallel", "arbitrary"), vmem_limit_bytes=VMEM_LIMIT),
        name="gdn",
    )(gq, gk, gv, gz, gba, alog_row, dtb_row, norm_w)


def _ffn_kernel(x_ref, yda_ref, ygdn_ref, wout_ref, fnw_ref, wup_ref, cw_ref, wdown_ref, finw_ref, o_ref,
                hx_refs, carry_ref, *, tm):
    n_chunks = D_FF // FF_CHUNK
    cw2 = 2 * FF_CHUNK

    @pl.when(pl.program_id(1) == 0)
    def _():
        carry_ref[...] = jnp.zeros_like(carry_ref)

    h = (x_ref[0] + _dot(yda_ref[0], wout_ref[0:GROUP_W, :])
         + _dot(ygdn_ref[0], wout_ref[GROUP_W:2 * GROUP_W, :]))
    hn = _rms(h, fnw_ref[...]).astype(BF16)
    acc = jnp.zeros((tm, D_MODEL), F32)

    halves = (slice(0, FF_CHUNK), slice(FF_CHUNK, cw2))

    def chunk_cols(ci):
        return (slice(ci * FF_CHUNK, (ci + 1) * FF_CHUNK), slice(D_FF + ci * FF_CHUNK, D_FF + (ci + 1) * FF_CHUNK))

    def up_proj(ci):
        hx_ref = hx_refs.at[ci % FF_WINDOWS]
        for half, cols in zip(halves, chunk_cols(ci)):
            hx_ref[0:8, half] = carry_ref[:, cols]
            hx_ref[8:8 + tm, half] = _dot(hn, wup_ref[:, cols])
            carry_ref[:, cols] = hx_ref[tm:tm + 8, half]

    for ci in range(FF_WINDOWS - 1):
        up_proj(ci)
    for ci in range(n_chunks):
        if ci + FF_WINDOWS - 1 < n_chunks:
            up_proj(ci + FF_WINDOWS - 1)
        hx_ref = hx_refs.at[ci % FF_WINDOWS]
        cw = jnp.concatenate([cw_ref[:, cols] for cols in chunk_cols(ci)], axis=-1)
        y = cw[FFN_CONV - 1:FFN_CONV] * hx_ref[8:8 + tm, :]
        for j in range(1, FFN_CONV):
            y = y + cw[FFN_CONV - 1 - j:FFN_CONV - j] * hx_ref[8 - j:8 - j + tm, :]
        hg = 0.5 * y[:, :FF_CHUNK]
        act = (hg * (1.0 + jnp.tanh(hg)) * y[:, FF_CHUNK:]).astype(BF16)
        acc = acc + _dot(act, wdown_ref[ci * FF_CHUNK:(ci + 1) * FF_CHUNK, :])
    o_ref[0] = _rms(h + acc, finw_ref[...])


def _ffn(x, y_da, y_gdn, w_out, ffn_norm_w, w_up, conv_w, w_down, final_norm_w, tm):
    b, t, d = x.shape
    tok_spec = lambda width: pl.BlockSpec((1, tm, width), lambda i, j: (i, j, 0))
    const = lambda shape: pl.BlockSpec(shape, lambda i, j: (0,) * len(shape),
                                       pipeline_mode=pl.Buffered(1))
    return pl.pallas_call(
        functools.partial(_ffn_kernel, tm=tm),
        grid=(b, t // tm),
        in_specs=[tok_spec(d), tok_spec(GROUP_W), tok_spec(GROUP_W), const(w_out.shape), const((1, d)),
                  const(w_up.shape), const(conv_w.shape), const(w_down.shape), const((1, d))],
        out_specs=tok_spec(d),
        out_shape=jax.ShapeDtypeStruct((b, t, d), F32),
        scratch_shapes=[pltpu.VMEM((FF_WINDOWS, 8 + tm, 2 * FF_CHUNK), F32),
                        pltpu.VMEM((8, 2 * D_FF), F32)],
        compiler_params=pltpu.CompilerParams(
            dimension_semantics=("parallel", "arbitrary"), vmem_limit_bytes=VMEM_LIMIT),
        name="outproj_ffn",
    )(x, y_da, y_gdn, w_out, ffn_norm_w, w_up, conv_w, w_down, final_norm_w)


def _permute_qk_cols(w):
    half = DA_HEAD_DIM // 2
    rows = w.shape[0]
    return w.reshape(rows, DA_HEADS, 2, 2, half).transpose(0, 1, 3, 2, 4).reshape(rows, GROUP_W)


def _rope_tables(t):
    half = DA_HEAD_DIM // 2
    inv_freq = ROPE_THETA ** (-jnp.arange(0, DA_HEAD_DIM, 2, dtype=F32) / DA_HEAD_DIM)
    ang = jnp.arange(t, dtype=F32)[:, None] * inv_freq[None, :]
    cos, sin = jnp.cos(ang), jnp.sin(ang)
    assert cos.shape == (t, half)
    return jnp.tile(cos, (1, 4)), jnp.concatenate([-sin, -sin, sin, sin], axis=-1)


def _tile(t, pref):
    while t % pref:
        pref //= 2
    return pref


def kernel(x, attn_norm_w, w_in, da_lambda_q1, da_lambda_k1, da_lambda_q2, da_lambda_k2, da_subln_w,
           gdn_conv_w, gdn_a_log, gdn_dt_bias, gdn_norm_w, w_out, ffn_norm_w, ffn_w_up, ffn_conv_w,
           ffn_w_down, final_norm_w):
    b, t, d = x.shape
    assert d == D_MODEL and t % GDN_CHUNK == 0
    l = 0
    w = w_in[l]
    nh = GDN_HEADS
    small = jnp.pad(w[:, 7 * GROUP_W:], ((0, 0), (0, HEAD_W - 2 * nh)))
    w_all = jnp.concatenate([_permute_qk_cols(w[:, :GROUP_W]), _permute_qk_cols(w[:, GROUP_W:2 * GROUP_W]),
                             w[:, 2 * GROUP_W:7 * GROUP_W], small], axis=-1).astype(BF16)
    cos_t, sin_t = _rope_tables(t)

    q, k, v, gq, gk, gv, gz, gba = _inproj(x, attn_norm_w[l][None, :], w_all, cos_t, sin_t, gdn_conv_w[l],
                                           _tile(t, 512))

    lam_p = jnp.stack([da_lambda_q1[l], da_lambda_k1[l], da_lambda_q2[l], da_lambda_k2[l]])
    y_da = _attention(q, k, v, lam_p, da_subln_w[l][None, :], _tile(t, 512))

    lane_pad = lambda vec: jnp.pad(vec, (nh, HEAD_W - 2 * nh))[None, :]
    y_gdn = _gdn(gq, gk, gv, gz, gba, lane_pad(gdn_a_log[l]), lane_pad(gdn_dt_bias[l]),
                 gdn_norm_w[l][None, :], _tile(t, 512), GDN_CHUNK)

    return _ffn(x, y_da, y_gdn, w_out[l].astype(BF16), ffn_norm_w[l][None, :],
                ffn_w_up[l].astype(BF16), ffn_conv_w[l],
                ffn_w_down[l].astype(BF16), final_norm_w[None, :], _tile(t, 512))
```

```python
import functools
import math

import jax
import jax.numpy as jnp
from jax import lax
from jax.experimental import pallas as pl
from jax.experimental.pallas import tpu as pltpu

F32 = jnp.float32
BF16 = jnp.bfloat16

D_MODEL = 1024
DA_HEADS = 4
DA_HEAD_DIM = 64
HEAD_W = 128
GROUP_W = 512
GDN_HEADS = 4
SHORT_CONV = 4
D_FF = 2816
FFN_CONV = 3
FF_CHUNK = 256
FF_WINDOWS = 6
ROPE_THETA = 10000.0
EPS = 1e-6
LAMBDA_INIT = 0.8 - 0.6 * math.exp(-0.3 * 0)
GDN_CHUNK = 128
INV_BLOCK = 16
GDN_GROUP = 4
NEG_BIG = -1e30
ATTN_AHEAD = 3
LOG2E = math.log2(math.e)
VMEM_LIMIT = 56 * 1024 * 1024


def _rms(x, w):
    return x * lax.rsqrt(jnp.mean(x * x, axis=-1, keepdims=True) + EPS) * w


def _dot(a, b):
    return jnp.dot(a, b, preferred_element_type=F32)


def _dot_nt(a, b):
    return lax.dot_general(a, b, (((1,), (1,)), ((), ())), preferred_element_type=F32)


def _sigmoid(x):
    return 1.0 / (1.0 + jnp.exp(-x))


def _silu(x):
    hx = 0.5 * x
    return hx * (1.0 + jnp.tanh(hx))


def _inproj_kernel(x_ref, nw_ref, w_ref, cos_ref, sin_ref, cw_ref,
                   q_ref, k_ref, v_ref, gq_ref, gk_ref, gv_ref, gz_ref, gba_ref,
                   win0_ref, win1_ref, win2_ref, carry_ref, *, tm):
    @pl.when(pl.program_id(1) == 0)
    def _():
        carry_ref[...] = jnp.zeros_like(carry_ref)

    xn = _rms(x_ref[0], nw_ref[...]).astype(BF16)
    cos = cos_ref[...]
    sin = sin_ref[...]

    def proj(seg, width=GROUP_W):
        return _dot(xn, w_ref[:, seg * GROUP_W: seg * GROUP_W + width])

    def rope(p, scale):
        outs = []
        for h in range(DA_HEADS):
            ph = p[:, h * HEAD_W:(h + 1) * HEAD_W]
            outs.append((ph * cos + pltpu.roll(ph, HEAD_W // 2, 1) * sin) * scale)
        return jnp.concatenate(outs, axis=-1)

    wins = (win0_ref, win1_ref, win2_ref)

    def fill_window(seg, slot):
        cols = slice(slot * GROUP_W, (slot + 1) * GROUP_W)
        win = wins[slot]
        win[0:8, :] = carry_ref[:, cols]
        win[8:8 + tm, :] = proj(seg)
        carry_ref[:, cols] = win[tm:tm + 8, :]

    def short_conv_silu(slot, out_ref, l2_scale):
        win = wins[slot]
        for h in range(GDN_HEADS):
            hc = slice(h * HEAD_W, (h + 1) * HEAD_W)
            cw = cw_ref[:, slot * GROUP_W + h * HEAD_W: slot * GROUP_W + (h + 1) * HEAD_W]
            y = cw[SHORT_CONV - 1:SHORT_CONV] * win[8:8 + tm, hc]
            for j in range(1, SHORT_CONV):
                y = y + cw[SHORT_CONV - 1 - j:SHORT_CONV - j] * win[8 - j:8 - j + tm, hc]
            y = _silu(y)
            if l2_scale is not None:
                y = y * (lax.rsqrt(jnp.sum(y * y, axis=-1, keepdims=True) + EPS) * l2_scale)
            out_ref[0, :, hc] = y.astype(BF16)

    fill_window(3, 0)
    v_ref[0] = proj(2).astype(BF16)
    fill_window(4, 1)
    q_ref[0] = rope(proj(0), DA_HEAD_DIM ** -0.5 * LOG2E).astype(BF16)
    short_conv_silu(0, gq_ref, HEAD_W ** -0.5)
    fill_window(5, 2)
    gz_ref[0] = proj(6).astype(BF16)
    k_ref[0] = rope(proj(1), 1.0).astype(BF16)
    short_conv_silu(1, gk_ref, 1.0)
    gba_ref[0] = proj(7, HEAD_W)
    short_conv_silu(2, gv_ref, None)


def _inproj(x, norm_w, w_all, cos_t, sin_t, conv_w, tm):
    b, t, d = x.shape
    grid = (b, t // tm)
    tok = lambda width, dt: jax.ShapeDtypeStruct((b, t, width), dt)
    tok_spec = lambda width: pl.BlockSpec((1, tm, width), lambda i, j: (i, j, 0))
    const = lambda shape: pl.BlockSpec(shape, lambda i, j: (0,) * len(shape),
                                       pipeline_mode=pl.Buffered(1))
    return pl.pallas_call(
        functools.partial(_inproj_kernel, tm=tm),
        grid=grid,
        in_specs=[tok_spec(d), const((1, d)), const(w_all.shape),
                  pl.BlockSpec((tm, HEAD_W), lambda i, j: (j, 0)),
                  pl.BlockSpec((tm, HEAD_W), lambda i, j: (j, 0)),
                  const(conv_w.shape)],
        out_specs=[tok_spec(GROUP_W)] * 7 + [tok_spec(HEAD_W)],
        out_shape=[tok(GROUP_W, BF16)] * 7 + [tok(HEAD_W, F32)],
        scratch_shapes=[pltpu.VMEM((8 + tm, GROUP_W), F32)] * 3 + [
                        pltpu.VMEM((8, 3 * GROUP_W), F32)],
        compiler_params=pltpu.CompilerParams(
            dimension_semantics=("parallel", "arbitrary"), vmem_limit_bytes=VMEM_LIMIT),
        name="inproj",
    )(x, norm_w, w_all, cos_t, sin_t, conv_w)


def _attn_kernel(q_ref, k_ref, v_ref, lam_ref, sw_ref, o_ref, m_ref, a_ref, *, tq, tk):
    i = pl.program_id(1)
    lane = lax.broadcasted_iota(jnp.int32, (1, HEAD_W), 1)
    first_map = (lane % 64) < 32
    head_cols = [slice(h * HEAD_W, (h + 1) * HEAD_W) for h in range(DA_HEADS)]
    streams = []
    for h in range(DA_HEADS):
        q = q_ref[0, :, head_cols[h]]
        zero = jnp.zeros_like(q)
        streams.append((h, 2 * h, jnp.where(first_map, q, zero)))
        streams.append((h, 2 * h + 1, jnp.where(first_map, zero, q)))
    def block(st, width, diag):
        for r0, nr in (((0, tq // 2), (tq // 2, tq // 2)) if diag else ((0, tq),)):
            rows = slice(r0, r0 + nr)
            ncols = width - tq + r0 + nr if diag else width
            kb = [k_ref[0, pl.ds(st, ncols), head_cols[h]] for h in range(DA_HEADS)]
            ones = jnp.ones((ncols, HEAD_W), BF16)
            vb = [jnp.concatenate([v_ref[0, pl.ds(st, ncols), head_cols[h]], ones], axis=-1) for h in range(DA_HEADS)]
            n_st = len(streams)
            scores = [_dot_nt(streams[j][2][rows], kb[streams[j][0]]) for j in range(min(ATTN_AHEAD, n_st))]
            for j, (h, si, _) in enumerate(streams):
                if j + ATTN_AHEAD < n_st:
                    nxt = streams[j + ATTN_AHEAD]
                    scores.append(_dot_nt(nxt[2][rows], kb[nxt[0]]))
                s = scores[j]
                if diag:
                    row = lax.broadcasted_iota(jnp.int32, (nr, ncols), 0) + (r0 + width - tq)
                    col = lax.broadcasted_iota(jnp.int32, (nr, ncols), 1)
                    s = jnp.where(col <= row, s, NEG_BIG)
                    m_new = jnp.broadcast_to(jnp.max(s, axis=-1, keepdims=True), (nr, HEAD_W))
                else:
                    m_old = m_ref[si, rows]
                    m_new = jnp.maximum(m_old, jnp.max(s, axis=-1, keepdims=True))
                p = jnp.concatenate([jnp.exp2((s[:, c * HEAD_W:(c + 1) * HEAD_W] - m_new).astype(BF16))
                                     for c in range(ncols // HEAD_W)], axis=-1)
                pv = _dot(p, vb[h])
                if diag:
                    a_ref[si, rows] = pv
                else:
                    alpha = jnp.exp2(m_old - m_new)
                    a_ref[si, rows] = jnp.concatenate([alpha, alpha], axis=-1) * a_ref[si, rows] + pv
                m_ref[si, rows] = m_new

    @pl.when(i % 2 == 1)
    def _():
        block(pl.multiple_of((i - 1) * tq, tk), tk, True)

    @pl.when(i % 2 == 0)
    def _():
        block(pl.multiple_of(i * tq, tk), tq, True)

    def body(j, carry):
        block(pl.multiple_of(j * tk, tk), tk, False)
        return carry

    lax.fori_loop(0, (i * tq) // tk, body, 0)

    lp = lam_ref[...]
    lam = (jnp.exp(jnp.sum(lp[0:1] * lp[1:2], axis=-1, keepdims=True))
           - jnp.exp(jnp.sum(lp[2:3] * lp[3:4], axis=-1, keepdims=True)) + LAMBDA_INIT)
    for h in range(DA_HEADS):
        a1 = a_ref[2 * h]
        a2 = a_ref[2 * h + 1]
        o = a1[:, :HEAD_W] * (1.0 / a1[:, HEAD_W:]) - lam * (a2[:, :HEAD_W] * (1.0 / a2[:, HEAD_W:]))
        o_ref[0, :, head_cols[h]] = (_rms(o, sw_ref[...]) * (1.0 - LAMBDA_INIT)).astype(BF16)


def _attention(q, k, v, lam_p, subln_w, tq):
    b, t, _ = q.shape
    tk = 2 * tq
    assert t % tq == 0 and tq % (2 * HEAD_W) == 0
    n_streams = 2 * DA_HEADS
    return pl.pallas_call(
        functools.partial(_attn_kernel, tq=tq, tk=tk),
        grid=(b, t // tq),
        in_specs=[pl.BlockSpec((1, tq, GROUP_W), lambda bi, i: (bi, i, 0)),
                  pl.BlockSpec((1, t, GROUP_W), lambda bi, i: (bi, 0, 0)),
                  pl.BlockSpec((1, t, GROUP_W), lambda bi, i: (bi, 0, 0)),
                  pl.BlockSpec(lam_p.shape, lambda bi, i: (0, 0)),
                  pl.BlockSpec((1, HEAD_W), lambda bi, i: (0, 0))],
        out_specs=pl.BlockSpec((1, tq, GROUP_W), lambda bi, i: (bi, i, 0)),
        out_shape=jax.ShapeDtypeStruct((b, t, GROUP_W), BF16),
        scratch_shapes=[pltpu.VMEM((n_streams, tq, HEAD_W), F32),
                        pltpu.VMEM((n_streams, tq, 2 * HEAD_W), F32)],
        compiler_params=pltpu.CompilerParams(
            dimension_semantics=("parallel", "arbitrary"), vmem_limit_bytes=VMEM_LIMIT),
        name="diff_attn",
    )(q, k, v, lam_p, subln_w)


def _gdn_kernel(gq_ref, gk_ref, gv_ref, gz_ref, gba_ref, alog_ref, dtb_ref, nw_ref, o_ref,
                s_ref, u_ref, w_ref, qk_ref, qd_ref, kdt_ref, gl_ref, *, tt, c):
    nh = GDN_HEADS
    nc = tt // c

    @pl.when(pl.program_id(1) == 0)
    def _():
        s_ref[...] = jnp.zeros_like(s_ref)

    gba = gba_ref[0]
    beta_all = _sigmoid(gba)
    xa = gba + dtb_ref[...]
    g_all = -jnp.exp(alog_ref[...]) * (jnp.maximum(xa, 0.0) + jnp.log1p(jnp.exp(-jnp.abs(xa))))

    r_i = lax.broadcasted_iota(jnp.int32, (c, c), 0)
    c_i = lax.broadcasted_iota(jnp.int32, (c, c), 1)
    tril = c_i <= r_i
    strict = c_i < r_i
    eye = (c_i == r_i).astype(F32)
    tril_f = tril.astype(F32)
    blk_mask = [(r_i // INV_BLOCK) == (c_i // INV_BLOCK)]
    size = INV_BLOCK
    while size < c:
        blk_mask.append(((r_i // (2 * size)) == (c_i // (2 * size))) & ((r_i // size) > (c_i // size)))
        size *= 2

    def level(fn, items):
        return [fn(x) for x in items]

    for g0 in range(0, nc, GDN_GROUP):
        prep = []
        for ci in range(g0, g0 + GDN_GROUP):
            rows = slice(ci * c, (ci + 1) * c)
            gcum = jnp.dot(tril_f, g_all[rows], preferred_element_type=F32, precision=lax.Precision.HIGHEST)
            gcum_t = gcum.T
            for h in range(nh):
                cols = slice(h * HEAD_W, (h + 1) * HEAD_W)
                q_bf = gq_ref[0, rows, cols]
                k_bf = gk_ref[0, rows, cols]
                q = q_bf.astype(F32)
                k = k_bf.astype(F32)
                v = gv_ref[0, rows, cols].astype(F32)
                beta = beta_all[rows, h:h + 1]
                gc = gcum[:, nh + h:nh + h + 1]
                gr = gcum_t[nh + h:nh + h + 1, :]
                g_last = gcum[c - 1:c, nh + h:nh + h + 1]
                decay = jnp.exp(jnp.where(tril, gc - gr, -jnp.inf))
                e_gc = jnp.exp(gc)
                kb = k * beta
                idx = ci * nh + h
                qd_ref[idx] = (q * e_gc).astype(BF16)
                kdt_ref[idx] = (k * jnp.exp(g_last - gc)).T.astype(BF16)
                gl_ref[idx] = jnp.broadcast_to(jnp.exp(g_last), (8, HEAD_W))
                rhs = jnp.concatenate([v * beta, kb * e_gc], axis=-1).astype(BF16)
                prep.append((idx, q_bf, k_bf, kb.astype(BF16), decay, rhs))
        kkt = level(lambda a: _dot_nt(a[3], a[2]) * a[4], prep)
        for a, qk in zip(prep, level(lambda a: _dot_nt(a[1], a[2]) * a[4], prep)):
            qk_ref[a[0]] = qk.astype(BF16)
        lmat = [jnp.where(strict, x, 0.0) for x in kkt]
        p = [jnp.where(blk_mask[0], -x, 0.0) for x in lmat]
        t_inv = [eye + x for x in p]
        for _ in range(int(math.log2(INV_BLOCK)) - 1):
            p = level(lambda x: _dot(x.astype(BF16), x.astype(BF16)), p)
            t_inv = level(lambda tx: tx[0] + _dot(tx[0].astype(BF16), tx[1].astype(BF16)), list(zip(t_inv, p)))
        for off_mask in blk_mask[1:]:
            t_bf = [x.astype(BF16) for x in t_inv]
            x_off = level(lambda tl: _dot(tl[0], jnp.where(off_mask, tl[1], 0.0).astype(BF16)).astype(BF16),
                          list(zip(t_bf, lmat)))
            t_inv = level(lambda txb: txb[0] - _dot(txb[1], txb[2]), list(zip(t_inv, x_off, t_bf)))
        for a, sol in zip(prep, level(lambda ta: _dot(ta[0].astype(BF16), ta[1][5]), list(zip(t_inv, prep)))):
            u_ref[a[0]] = sol[:, :HEAD_W]
            w_ref[a[0]] = sol[:, HEAD_W:].astype(BF16)

    nw = nw_ref[...]
    heads = range(nh)
    for ci in range(nc):
        rows = slice(ci * c, (ci + 1) * c)
        s_old = [s_ref[h] for h in heads]
        s_bf = [x.astype(BF16) for x in s_old]
        ws = [_dot(w_ref[ci * nh + h], s_bf[h]) for h in heads]
        qs = [_dot(qd_ref[ci * nh + h], s_bf[h]) for h in heads]
        v_bf = [(u_ref[ci * nh + h] - ws[h]).astype(BF16) for h in heads]
        kv = [_dot(kdt_ref[ci * nh + h], v_bf[h]) for h in heads]
        for h in heads:
            s_ref[h] = s_old[h] * gl_ref[ci * nh + h][0:1, :] + kv[h]
        o = [qs[h] + _dot(qk_ref[ci * nh + h], v_bf[h]) for h in heads]
        for h in heads:
            cols = slice(h * HEAD_W, (h + 1) * HEAD_W)
            z = gz_ref[0, rows, cols].astype(F32)
            o_ref[0, rows, cols] = (_rms(o[h], nw) * _silu(z)).astype(BF16)


def _gdn(gq, gk, gv, gz, gba, alog_row, dtb_row, norm_w, tt, c):
    b, t, _ = gq.shape
    nidx = (tt // c) * GDN_HEADS
    tok_spec = lambda width: pl.BlockSpec((1, tt, width), lambda i, j: (i, j, 0))
    const = lambda shape: pl.BlockSpec(shape, lambda i, j: (0,) * len(shape))
    return pl.pallas_call(
        functools.partial(_gdn_kernel, tt=tt, c=c),
        grid=(b, t // tt),
        in_specs=[tok_spec(GROUP_W)] * 4 + [tok_spec(HEAD_W)] + [const((1, HEAD_W))] * 3,
        out_specs=tok_spec(GROUP_W),
        out_shape=jax.ShapeDtypeStruct((b, t, GROUP_W), BF16),
        scratch_shapes=[pltpu.VMEM((GDN_HEADS, HEAD_W, HEAD_W), F32),
                        pltpu.VMEM((nidx, c, HEAD_W), F32),
                        pltpu.VMEM((nidx, c, HEAD_W), BF16),
                        pltpu.VMEM((nidx, c, c), BF16),
                        pltpu.VMEM((nidx, c, HEAD_W), BF16),
                        pltpu.VMEM((nidx, HEAD_W, c), BF16),
                        pltpu.VMEM((nidx, 8, HEAD_W), F32)],
        compiler_params=pltpu.CompilerParams(
            dimension_semantics=("parallel", "arbitrary"), vmem_limit_bytes=VMEM_LIMIT),
        name="gdn",
    )(gq, gk, gv, gz, gba, alog_row, dtb_row, norm_w)


def _ffn_kernel(x_ref, yda_ref, ygdn_ref, wout_ref, fnw_ref, wup_ref, cw_ref, wdown_ref, finw_ref, o_ref,
                hx_refs, carry_ref, *, tm):
    n_chunks = D_FF // FF_CHUNK
    cw2 = 2 * FF_CHUNK

    @pl.when(pl.program_id(1) == 0)
    def _():
        carry_ref[...] = jnp.zeros_like(carry_ref)

    h = (x_ref[0] + _dot(yda_ref[0], wout_ref[0:GROUP_W, :])
         + _dot(ygdn_ref[0], wout_ref[GROUP_W:2 * GROUP_W, :]))
    hn = _rms(h, fnw_ref[...]).astype(BF16)
    acc = jnp.zeros((tm, D_MODEL), F32)

    halves = (slice(0, FF_CHUNK), slice(FF_CHUNK, cw2))

    def chunk_cols(ci):
        return (slice(ci * FF_CHUNK, (ci + 1) * FF_CHUNK), slice(D_FF + ci * FF_CHUNK, D_FF + (ci + 1) * FF_CHUNK))

    def up_proj(ci):
        hx_ref = hx_refs.at[ci % FF_WINDOWS]
        for half, cols in zip(halves, chunk_cols(ci)):
            hx_ref[0:8, half] = carry_ref[:, cols]
            hx_ref[8:8 + tm, half] = _dot(hn, wup_ref[:, cols])
            carry_ref[:, cols] = hx_ref[tm:tm + 8, half]

    for ci in range(FF_WINDOWS - 1):
        up_proj(ci)
    for ci in range(n_chunks):
        if ci + FF_WINDOWS - 1 < n_chunks:
            up_proj(ci + FF_WINDOWS - 1)
        hx_ref = hx_refs.at[ci % FF_WINDOWS]
        cw = jnp.concatenate([cw_ref[:, cols] for cols in chunk_cols(ci)], axis=-1)
        y = cw[FFN_CONV - 1:FFN_CONV] * hx_ref[8:8 + tm, :]
        for j in range(1, FFN_CONV):
            y = y + cw[FFN_CONV - 1 - j:FFN_CONV - j] * hx_ref[8 - j:8 - j + tm, :]
        hg = 0.5 * y[:, :FF_CHUNK]
        act = (hg * (1.0 + jnp.tanh(hg)) * y[:, FF_CHUNK:]).astype(BF16)
        acc = acc + _dot(act, wdown_ref[ci * FF_CHUNK:(ci + 1) * FF_CHUNK, :])
    o_ref[0] = _rms(h + acc, finw_ref[...])


def _ffn(x, y_da, y_gdn, w_out, ffn_norm_w, w_up, conv_w, w_down, final_norm_w, tm):
    b, t, d = x.shape
    tok_spec = lambda width: pl.BlockSpec((1, tm, width), lambda i, j: (i, j, 0))
    const = lambda shape: pl.BlockSpec(shape, lambda i, j: (0,) * len(shape),
                                       pipeline_mode=pl.Buffered(1))
    return pl.pallas_call(
        functools.partial(_ffn_kernel, tm=tm),
        grid=(b, t // tm),
        in_specs=[tok_spec(d), tok_spec(GROUP_W), tok_spec(GROUP_W), const(w_out.shape), const((1, d)),
                  const(w_up.shape), const(conv_w.shape), const(w_down.shape), const((1, d))],
        out_specs=tok_spec(d),
        out_shape=jax.ShapeDtypeStruct((b, t, d), F32),
        scratch_shapes=[pltpu.VMEM((FF_WINDOWS, 8 + tm, 2 * FF_CHUNK), F32),
                        pltpu.VMEM((8, 2 * D_FF), F32)],
        compiler_params=pltpu.CompilerParams(
            dimension_semantics=("parallel", "arbitrary"), vmem_limit_bytes=VMEM_LIMIT),
        name="outproj_ffn",
    )(x, y_da, y_gdn, w_out, ffn_norm_w, w_up, conv_w, w_down, final_norm_w)


def _permute_qk_cols(w):
    half = DA_HEAD_DIM // 2
    rows = w.shape[0]
    return w.reshape(rows, DA_HEADS, 2, 2, half).transpose(0, 1, 3, 2, 4).reshape(rows, GROUP_W)


def _rope_tables(t):
    half = DA_HEAD_DIM // 2
    inv_freq = ROPE_THETA ** (-jnp.arange(0, DA_HEAD_DIM, 2, dtype=F32) / DA_HEAD_DIM)
    ang = jnp.arange(t, dtype=F32)[:, None] * inv_freq[None, :]
    cos, sin = jnp.cos(ang), jnp.sin(ang)
    assert cos.shape == (t, half)
    return jnp.tile(cos, (1, 4)), jnp.concatenate([-sin, -sin, sin, sin], axis=-1)


def _tile(t, pref):
    while t % pref:
        pref //= 2
    return pref


def kernel(x, attn_norm_w, w_in, da_lambda_q1, da_lambda_k1, da_lambda_q2, da_lambda_k2, da_subln_w,
           gdn_conv_w, gdn_a_log, gdn_dt_bias, gdn_norm_w, w_out, ffn_norm_w, ffn_w_up, ffn_conv_w,
           ffn_w_down, final_norm_w):
    b, t, d = x.shape
    assert d == D_MODEL and t % GDN_CHUNK == 0
    l = 0
    w = w_in[l]
    nh = GDN_HEADS
    small = jnp.pad(w[:, 7 * GROUP_W:], ((0, 0), (0, HEAD_W - 2 * nh)))
    w_all = jnp.concatenate([_permute_qk_cols(w[:, :GROUP_W]), _permute_qk_cols(w[:, GROUP_W:2 * GROUP_W]),
                             w[:, 2 * GROUP_W:7 * GROUP_W], small], axis=-1).astype(BF16)
    cos_t, sin_t = _rope_tables(t)

    q, k, v, gq, gk, gv, gz, gba = _inproj(x, attn_norm_w[l][None, :], w_all, cos_t, sin_t, gdn_conv_w[l],
                                           _tile(t, 512))

    lam_p = jnp.stack([da_lambda_q1[l], da_lambda_k1[l], da_lambda_q2[l], da_lambda_k2[l]])
    y_da = _attention(q, k, v, lam_p, da_subln_w[l][None, :], _tile(t, 512))

    lane_pad = lambda vec: jnp.pad(vec, (nh, HEAD_W - 2 * nh))[None, :]
    y_gdn = _gdn(gq, gk, gv, gz, gba, lane_pad(gdn_a_log[l]), lane_pad(gdn_dt_bias[l]),
                 gdn_norm_w[l][None, :], _tile(t, 512), GDN_CHUNK)

    return _ffn(x, y_da, y_gdn, w_out[l].astype(BF16), ffn_norm_w[l][None, :],
                ffn_w_up[l].astype(BF16), ffn_conv_w[l],
                ffn_w_down[l].astype(BF16), final_norm_w[None, :], _tile(t, 512))
```
